```python
import math
import jax, jax.numpy as jnp
from jax import lax
import numpy as np

D_MODEL = 1024
BATCH = 16
SEQ = 2048
DEPTH = 4

HEAD_DIM = 128
ROT_DIM = HEAD_DIM // 4
ROPE_THETA = 500000.0
DIL_PAIRS = ((128, 1), (512, 4), (2048, 16))
N_GROUPS = len(DIL_PAIRS)
A_HEADS = D_MODEL // HEAD_DIM
A_WIDTH = A_HEADS * HEAD_DIM
A_QKV = N_GROUPS * A_WIDTH
B_HEADS = D_MODEL // HEAD_DIM
B_KEY = 128
B_VAL = 128
B_WIDTH = B_HEADS * B_VAL
CHUNK = 64
MEM_LEN = 256
MEM_HEADS = 4
MEM_WIDTH = MEM_HEADS * HEAD_DIM
MIX_WIDTH = A_WIDTH + MEM_WIDTH
IN_A = 3 * A_QKV + MEM_WIDTH + MIX_WIDTH
IN_B = 3 * B_HEADS * B_KEY + MEM_WIDTH + MIX_WIDTH
N_A_LAYERS = (DEPTH + 1) // 2
N_B_LAYERS = DEPTH // 2
EPS = 1e-6
ATTN_SCALE = 1.0 / math.sqrt(HEAD_DIM)

kernel_name = "hybrid_dilated_attn_hgrn2_memxattn"


def rms_norm(x, g):
    xf = x.astype(jnp.float32)
    y = xf * lax.rsqrt(jnp.mean(xf * xf, axis=-1, keepdims=True) + EPS)
    return (y * g.astype(jnp.float32)).astype(x.dtype)


def rotary_tables(positions):
    inv_freq = ROPE_THETA ** (-jnp.arange(0, ROT_DIM, 2, dtype=jnp.float32) / ROT_DIM)
    ang = positions.astype(jnp.float32)[..., None] * inv_freq
    return jnp.cos(ang), jnp.sin(ang)


def apply_partial_rotary(x, cos, sin):
    xf = x.astype(jnp.float32)
    half = ROT_DIM // 2
    x1, x2, rest = xf[..., :half], xf[..., half:ROT_DIM], xf[..., ROT_DIM:]
    out = jnp.concatenate([x1 * cos - x2 * sin, x2 * cos + x1 * sin, rest], axis=-1)
    return out.astype(x.dtype)


def dilated_window_attention(q, k, v, dilation, n_back):
    B, S, H, Dh = q.shape
    span = dilation * n_back
    Sp = -(-S // span) * span
    nb = Sp // span

    def to_blocks(t):
        t = jnp.pad(t, ((0, 0), (0, Sp - S), (0, 0), (0, 0)))
        return t.reshape(B, nb, n_back, dilation, H, Dh).transpose(0, 3, 1, 2, 4, 5)

    qb, kb, vb = to_blocks(q), to_blocks(k), to_blocks(v)
    shift = lambda t: jnp.pad(t, ((0, 0), (0, 0), (1, 0), (0, 0), (0, 0), (0, 0)))[:, :, :-1]
    kc = jnp.concatenate([shift(kb), kb], axis=3)
    vc = jnp.concatenate([shift(vb), vb], axis=3)
    s = jnp.einsum('brcqhd,brckhd->brchqk', qb, kc).astype(jnp.float32) * ATTN_SCALE
    i = jnp.arange(n_back)[:, None]
    j = jnp.arange(2 * n_back)[None, :]
    dist = n_back + i - j
    band = (dist >= 0) & (dist <= n_back)
    c = jnp.arange(nb)[:, None, None]
    valid = band[None] & ((c * n_back + j[None] - n_back) >= 0)
    valid = valid[None, None, :, None]
    s = jnp.where(valid, s, -jnp.inf)
    m = jnp.max(s, axis=-1, keepdims=True)
    p = jnp.exp(s - m)
    den = jnp.sum(p, axis=-1, keepdims=True)
    o = jnp.einsum('brchqk,brckhd->brcqhd', p.astype(v.dtype), vc).astype(jnp.float32)
    o = o / jnp.moveaxis(den, 3, 4)
    lse = (m + jnp.log(den))[..., 0]
    o = o.transpose(0, 2, 3, 1, 4, 5).reshape(B, Sp, H, Dh)[:, :S]
    lse = lse.transpose(0, 2, 4, 1, 3).reshape(B, Sp, H)[:, :S]
    return o, lse


def dilated_mixer(cols, cos, sin, q_gain, k_gain):
    B, S, _ = cols.shape
    qkv = cols.reshape(B, S, 3, N_GROUPS, A_HEADS, HEAD_DIM)
    q = rms_norm(qkv[:, :, 0], q_gain[:, None, :])
    k = rms_norm(qkv[:, :, 1], k_gain[:, None, :])
    v = qkv[:, :, 2]
    cos5, sin5 = cos[:, :, None, None, :], sin[:, :, None, None, :]
    q = apply_partial_rotary(q, cos5, sin5)
    k = apply_partial_rotary(k, cos5, sin5)
    outs, lses = [], []
    for g, (window, dil) in enumerate(DIL_PAIRS):
        o, lse = dilated_window_attention(q[:, :, g], k[:, :, g], v[:, :, g], dil, window // dil)
        outs.append(o)
        lses.append(lse)
    alpha = jax.nn.softmax(jnp.stack(lses, axis=0), axis=0)
    o = jnp.sum(alpha[..., None] * jnp.stack(outs, axis=0), axis=0)
    return o.reshape(B, S, A_WIDTH).astype(cols.dtype)


def gla_chunk_scan(q, k, v, log_f):
    B, S, H, dk = q.shape
    dv = v.shape[-1]
    nc = S // CHUNK
    to_c = lambda t: t.reshape(B, nc, CHUNK, H, t.shape[-1]).transpose(1, 0, 3, 2, 4)
    qc, kc, vc = to_c(q), to_c(k), to_c(v)
    G = jnp.cumsum(to_c(log_f), axis=3)
    causal = jnp.tril(jnp.ones((CHUNK, CHUNK), dtype=bool))[..., None]

    def step(state, inp):
        qt, kt, vt, Gt = inp
        o_inter = jnp.einsum('bhtk,bhkv->bhtv', qt * jnp.exp(Gt), state)
        diff = Gt[:, :, :, None, :] - Gt[:, :, None, :, :]
        decay = jnp.where(causal, jnp.exp(jnp.where(causal, diff, 0.0)), 0.0)
        attn = jnp.einsum('bhtk,bhsk,bhtsk->bhts', qt, kt, decay)
        o_intra = jnp.einsum('bhts,bhsv->bhtv', attn, vt)
        g_last = Gt[:, :, -1:, :]
        k_dec = kt * jnp.exp(g_last - Gt)
        new_state = jnp.exp(g_last[:, :, 0, :])[..., None] * state + jnp.einsum('bhsk,bhsv->bhkv', k_dec, vt)
        return new_state, o_inter + o_intra

    init = jnp.zeros((B, H, dk, dv), jnp.float32)
    _, o = lax.scan(step, init, (qc, kc, vc, G))
    return o.transpose(1, 0, 3, 2, 4).reshape(B, S, H, dv)


def hgrn2_mixer(cols, lb, o_gain):
    B, S, _ = cols.shape
    w = B_HEADS * B_KEY
    q = cols[..., :w].astype(jnp.float32)
    f = cols[..., w:2 * w].astype(jnp.float32)
    iv = cols[..., 2 * w:].astype(jnp.float32)
    log_f = jnp.logaddexp(jnp.log(lb), jnp.log1p(-lb) + jax.nn.log_sigmoid(f))
    k = (1.0 - lb) * jax.nn.sigmoid(-f)
    sh = lambda t, d: t.reshape(B, S, B_HEADS, d)
    o = gla_chunk_scan(sh(q, B_KEY), sh(k, B_KEY), sh(iv, B_VAL), sh(log_f, B_KEY))
    o = rms_norm(o, o_gain)
    return o.reshape(B, S, B_WIDTH).astype(cols.dtype)


def memory_cross_attention(q_cols, mem, mem_gain, w_kv, q_gain, k_gain):
    B, S, _ = q_cols.shape
    M = mem.shape[1]
    kv = rms_norm(mem, mem_gain) @ w_kv
    km = rms_norm(kv[..., :MEM_WIDTH].reshape(B, M, MEM_HEADS, HEAD_DIM), k_gain)
    vm = kv[..., MEM_WIDTH:].reshape(B, M, MEM_HEADS, HEAD_DIM)
    qm = rms_norm(q_cols.reshape(B, S, MEM_HEADS, HEAD_DIM), q_gain)
    s = jnp.einsum('bshd,bmhd->bhsm', qm, km).astype(jnp.float32) * ATTN_SCALE
    p = jax.nn.softmax(s, axis=-1)
    o = jnp.einsum('bhsm,bmhd->bshd', p.astype(vm.dtype), vm)
    return o.reshape(B, S, MEM_WIDTH)


def setup_inputs(seed: int = 0) -> dict:
    key = jax.random.key(seed)
    ks = jax.random.split(key, 16)
    nrm = lambda k, shape, scale: jax.random.normal(k, shape, jnp.float32) * scale
    gain = lambda k, shape: 1.0 + 0.05 * jax.random.normal(k, shape, jnp.float32)
    x = nrm(ks[0], (BATCH, SEQ, D_MODEL), 1.0)
    mem = nrm(ks[1], (BATCH, MEM_LEN, D_MODEL), 1.0)
    offsets = jax.random.randint(ks[2], (BATCH, 1), 0, 4096, dtype=jnp.int32)
    positions = offsets + jnp.arange(SEQ, dtype=jnp.int32)[None, :]
    return {
        "x": x,
        "mem": mem,
        "positions": positions,
        "norm_gain": gain(ks[3], (DEPTH, D_MODEL)),
        "w_in_a": nrm(ks[4], (N_A_LAYERS, D_MODEL, IN_A), D_MODEL ** -0.5),
        "q_gain_a": gain(ks[5], (N_A_LAYERS, N_GROUPS, HEAD_DIM)),
        "k_gain_a": gain(ks[6], (N_A_LAYERS, N_GROUPS, HEAD_DIM)),
        "w_out_a": nrm(ks[7], (N_A_LAYERS, MIX_WIDTH, D_MODEL), MIX_WIDTH ** -0.5),
        "w_in_b": nrm(ks[8], (N_B_LAYERS, D_MODEL, IN_B), D_MODEL ** -0.5),
        "lb_logits": nrm(ks[9], (DEPTH, B_HEADS * B_KEY), 0.5),
        "o_gain_b": gain(ks[10], (N_B_LAYERS, B_VAL)),
        "w_out_b": nrm(ks[11], (N_B_LAYERS, MIX_WIDTH, D_MODEL), MIX_WIDTH ** -0.5),
        "mem_norm_gain": gain(ks[12], (DEPTH, D_MODEL)),
        "w_mem_kv": nrm(ks[13], (DEPTH, D_MODEL, 2 * MEM_WIDTH), D_MODEL ** -0.5),
        "mem_q_gain": gain(ks[14], (DEPTH, HEAD_DIM)),
        "mem_k_gain": gain(ks[15], (DEPTH, HEAD_DIM)),
    }


def reference(x, mem, positions, norm_gain, w_in_a, q_gain_a, k_gain_a, w_out_a,
              w_in_b, lb_logits, o_gain_b, w_out_b, mem_norm_gain, w_mem_kv,
              mem_q_gain, mem_k_gain):
    cos, sin = rotary_tables(positions)
    sm = jax.nn.softmax(lb_logits.astype(jnp.float32), axis=0)
    lower_bounds = jnp.cumsum(sm, axis=0) - sm[0:1]
    for l in range(DEPTH):
        j = l // 2
        h = rms_norm(x, norm_gain[l])
        if l % 2 == 0:
            cols = h @ w_in_a[j]
            mix = dilated_mixer(cols[..., :3 * A_QKV], cos, sin, q_gain_a[j], k_gain_a[j])
            q_mem = cols[..., 3 * A_QKV:3 * A_QKV + MEM_WIDTH]
            gate = cols[..., 3 * A_QKV + MEM_WIDTH:]
            w_out = w_out_a[j]
        else:
            cols = h @ w_in_b[j]
            wb = 3 * B_HEADS * B_KEY
            mix = hgrn2_mixer(cols[..., :wb], lower_bounds[l], o_gain_b[j])
            q_mem = cols[..., wb:wb + MEM_WIDTH]
            gate = cols[..., wb + MEM_WIDTH:]
            w_out = w_out_b[j]
        mem_out = memory_cross_attention(q_mem, mem, mem_norm_gain[l], w_mem_kv[l],
                                         mem_q_gain[l], mem_k_gain[l])
        y = jnp.concatenate([mix, mem_out], axis=-1) * jax.nn.silu(gate)
        x = x + y @ w_out
    return x
```

```python
import functools
import math

import numpy as np
import jax
import jax.numpy as jnp
from jax import lax
from jax.experimental import pallas as pl
from jax.experimental.pallas import tpu as pltpu

F32 = jnp.float32
BF16 = jnp.bfloat16

HEAD_DIM = 128
ROT_DIM = HEAD_DIM // 4
ROT_HALF = ROT_DIM // 2
ROPE_THETA = 500000.0
DIL_PAIRS = ((128, 1), (512, 4), (2048, 16))
N_GROUPS = len(DIL_PAIRS)
N_BACK = 128
MEM_HEADS = 4
MEM_WIDTH = MEM_HEADS * HEAD_DIM
EPS = 1e-6
ATTN_SCALE = 1.0 / math.sqrt(HEAD_DIM)
NEG_BIG = -1e30
HGRN_CHUNK = 128
VMEM_LIMIT = 52 * 1024 * 1024

_NT = (((1,), (1,)), ((), ()))


def _dot(a, b):
    return jnp.dot(a, b, preferred_element_type=F32)


def _dot_nt(a, b):
    return lax.dot_general(a, b, _NT, preferred_element_type=F32)


def _rms(x):
    return x * lax.rsqrt(jnp.mean(x * x, axis=-1, keepdims=True) + EPS)


def _silu(g):
    return g * (1.0 / (1.0 + jnp.exp(-g)))


def _params(sem):
    return pltpu.CompilerParams(dimension_semantics=sem, vmem_limit_bytes=VMEM_LIMIT)


def _prologue_kernel(x_ref, pos_ref, g_ref, freq_ref, sgn_ref, h_ref, cos_ref, sin_ref):
    h_ref[0] = (_rms(x_ref[0]) * g_ref[...]).astype(BF16)
    ang = pos_ref[0] * freq_ref[...]
    cos_ref[0] = jnp.cos(ang)
    sin_ref[0] = jnp.sin(ang) * sgn_ref[...]


def _prologue(x, positions, gain0):
    B, S, D = x.shape
    ts = 512
    inv_freq = ROPE_THETA ** (-jnp.arange(0, ROT_DIM, 2, dtype=F32) / ROT_DIM)
    freq = jnp.concatenate([inv_freq, inv_freq, jnp.zeros((HEAD_DIM - ROT_DIM,), F32)])[None, :]
    sgn = jnp.asarray(np.where(np.arange(HEAD_DIM) < ROT_HALF, -1.0, 1.0), F32)[None, :]
    pos = positions.astype(F32)[..., None]
    return pl.pallas_call(
        _prologue_kernel,
        grid=(B, S // ts),
        in_specs=[
            pl.BlockSpec((1, ts, D), lambda b, t: (b, t, 0)),
            pl.BlockSpec((1, ts, 1), lambda b, t: (b, t, 0)),
            pl.BlockSpec((1, D), lambda b, t: (0, 0)),
            pl.BlockSpec((1, HEAD_DIM), lambda b, t: (0, 0)),
            pl.BlockSpec((1, HEAD_DIM), lambda b, t: (0, 0)),
        ],
        out_specs=[
            pl.BlockSpec((1, ts, D), lambda b, t: (b, t, 0)),
            pl.BlockSpec((1, ts, HEAD_DIM), lambda b, t: (b, t, 0)),
            pl.BlockSpec((1, ts, HEAD_DIM), lambda b, t: (b, t, 0)),
        ],
        out_shape=[
            jax.ShapeDtypeStruct((B, S, D), BF16),
            jax.ShapeDtypeStruct((B, S, HEAD_DIM), F32),
            jax.ShapeDtypeStruct((B, S, HEAD_DIM), F32),
        ],
        compiler_params=_params(("parallel", "parallel")),
        name="prologue",
    )(x, pos, gain0[None, :], freq, sgn)


def _memkv_kernel(mem_ref, mg_ref, w_ref, kg_ref, km_ref, vm_ref):
    mn = (_rms(mem_ref[0]) * mg_ref[0]).astype(BF16)
    kv = _dot(mn, w_ref[0])
    for hd in range(MEM_HEADS):
        sl = slice(hd * HEAD_DIM, (hd + 1) * HEAD_DIM)
        km_ref[0, 0, :, sl] = (_rms(kv[:, sl]) * kg_ref[0]).astype(BF16)
    vm_ref[0, 0] = kv[:, MEM_WIDTH:].astype(BF16)


def _memkv(mem, mem_norm_gain, w_mem_kv, mem_k_gain):
    B, M, D = mem.shape
    depth = w_mem_kv.shape[0]
    out = jax.ShapeDtypeStruct((depth, B, M, MEM_WIDTH), BF16)
    return pl.pallas_call(
        _memkv_kernel,
        grid=(depth, B),
        in_specs=[
            pl.BlockSpec((1, M, D), lambda l, b: (b, 0, 0)),
            pl.BlockSpec((1, 1, D), lambda l, b: (l, 0, 0)),
            pl.BlockSpec((1, D, 2 * MEM_WIDTH), lambda l, b: (l, 0, 0)),
            pl.BlockSpec((1, 1, HEAD_DIM), lambda l, b: (l, 0, 0)),
        ],
        out_specs=[
            pl.BlockSpec((1, 1, M, MEM_WIDTH), lambda l, b: (l, b, 0, 0)),
            pl.BlockSpec((1, 1, M, MEM_WIDTH), lambda l, b: (l, b, 0, 0)),
        ],
        out_shape=[out, out],
        compiler_params=_params(("parallel", "parallel")),
        name="memkv",
    )(mem, mem_norm_gain[:, None, :], w_mem_kv.astype(BF16), mem_k_gain[:, None, :])


def _memattn_kernel(h_ref, w_ref, km_ref, vm_ref, qg_ref, y_ref):
    ht = h_ref[0]
    q_all = _dot(ht, w_ref[:, :MEM_WIDTH])
    gate = _dot(ht, w_ref[:, MEM_WIDTH:])
    for hd in range(MEM_HEADS):
        sl = slice(hd * HEAD_DIM, (hd + 1) * HEAD_DIM)
        qn = (_rms(q_all[:, sl]) * (qg_ref[...] * ATTN_SCALE)).astype(BF16)
        s = _dot_nt(qn, km_ref[0, 0, :, sl])
        m = jnp.max(s, axis=-1, keepdims=True)
        p = jnp.exp(s - m)
        den = jnp.sum(p, axis=-1, keepdims=True)
        o = _dot(p.astype(BF16), vm_ref[0, 0, :, sl]) * (1.0 / den)
        y_ref[0, :, sl] = (o * _silu(gate[:, sl])).astype(BF16)


def _memattn(h, w_mem, km, vm, layer, q_gain):
    B, S, D = h.shape
    M = km.shape[2]
    ts = 512
    return pl.pallas_call(
        _memattn_kernel,
        grid=(B, S // ts),
        in_specs=[
            pl.BlockSpec((1, ts, D), lambda b, t: (b, t, 0)),
            pl.BlockSpec((D, 2 * MEM_WIDTH), lambda b, t: (0, 0)),
            pl.BlockSpec((1, 1, M, MEM_WIDTH), lambda b, t: (layer, b, 0, 0)),
            pl.BlockSpec((1, 1, M, MEM_WIDTH), lambda b, t: (layer, b, 0, 0)),
            pl.BlockSpec((1, HEAD_DIM), lambda b, t: (0, 0)),
        ],
        out_specs=pl.BlockSpec((1, ts, MEM_WIDTH), lambda b, t: (b, t, 0)),
        out_shape=jax.ShapeDtypeStruct((B, S, MEM_WIDTH), BF16),
        compiler_params=_params(("parallel", "parallel")),
        name="memattn",
    )(h, w_mem, km, vm, q_gain[None, :])


def _rotary(x, cos, sin):
    lane = lax.broadcasted_iota(jnp.int32, x.shape, 1)
    swapped = jnp.where(lane < ROT_HALF,
                        pltpu.roll(x, HEAD_DIM - ROT_HALF, 1),
                        pltpu.roll(x, ROT_HALF, 1))
    return x * cos + swapped * sin


def _softmax_block(q, k, v, valid):
    s = jnp.where(valid, _dot_nt(q, k), NEG_BIG)
    m = jnp.max(s, axis=-1, keepdims=True)
    p = jnp.exp(s - m)
    den = jnp.sum(p, axis=-1, keepdims=True)
    o = _dot(p.astype(BF16), v) * (1.0 / den)
    return o, m + jnp.log(den)


def _attn_kernel(h_ref, w_ref, cos_ref, sin_ref, qg_ref, kg_ref, y_ref,
                 q_scr, k0_scr, k1_scr, k2_scr, v0_scr, v1_scr, v2_scr,
                 o_scr, l_scr, gate_scr, *, seq):
    k_scrs = (k0_scr, k1_scr, k2_scr)
    v_scrs = (v0_scr, v1_scr, v2_scr)
    pads = tuple(N_BACK * d if seq // (N_BACK * d) > 1 else 0 for _, d in DIL_PAIRS)
    tm = 256

    for g in range(N_GROUPS):
        if pads[g]:
            k_scrs[g][pl.ds(0, pads[g]), :] = jnp.zeros((pads[g], HEAD_DIM), F32)
            v_scrs[g][pl.ds(0, pads[g]), :] = jnp.zeros((pads[g], HEAD_DIM), F32)

    def proj_body(t, carry):
        r0 = pl.multiple_of(t * tm, tm)
        ht = h_ref[0, pl.ds(r0, tm), :]
        cos = cos_ref[0, pl.ds(r0, tm), :]
        sin = sin_ref[0, pl.ds(r0, tm), :]
        for pair in range(5):
            cols = _dot(ht, w_ref[:, pair * 256:(pair + 1) * 256])
            for half in range(2):
                blk = 2 * pair + half
                c = cols[:, half * HEAD_DIM:(half + 1) * HEAD_DIM]
                if blk < 3:
                    g = blk
                    qn = _rms(c) * (qg_ref[g:g + 1, :] * ATTN_SCALE)
                    q_scr[g, pl.ds(r0, tm), :] = _rotary(qn, cos, sin)
                elif blk < 6:
                    g = blk - 3
                    kn = _rms(c) * kg_ref[g:g + 1, :]
                    k_scrs[g][pl.ds(pads[g] + r0, tm), :] = _rotary(kn, cos, sin)
                elif blk < 9:
                    g = blk - 6
                    v_scrs[g][pl.ds(pads[g] + r0, tm), :] = c
                else:
                    gate_scr[pl.ds(r0, tm), :] = c
        return carry

    lax.fori_loop(0, seq // tm, proj_body, 0)

    for g, (_, d) in enumerate(DIL_PAIRS):
        nb = seq // (N_BACK * d)

        def rows(start, n, d=d):
            return pl.ds(start, n) if d == 1 else pl.ds(start, n, stride=d)

        if nb > 1:
            ii = lax.broadcasted_iota(jnp.int32, (N_BACK, 2 * N_BACK), 0)
            jj = lax.broadcasted_iota(jnp.int32, (N_BACK, 2 * N_BACK), 1)

            def blk_body(idx, carry, g=g, d=d, nb=nb, rows=rows, ii=ii, jj=jj):
                r = idx // nb
                c = idx % nb
                start = c * (N_BACK * d) + r
                q = q_scr[g, rows(start, N_BACK), :].astype(BF16)
                k = k_scrs[g][rows(start, 2 * N_BACK), :].astype(BF16)
                v = v_scrs[g][rows(start, 2 * N_BACK), :].astype(BF16)
                jmin = jnp.where(c == 0, N_BACK, 0)
                valid = (jj >= ii) & (jj <= ii + N_BACK) & (jj >= jmin)
                o, lse = _softmax_block(q, k, v, valid)
                o_scr[g, rows(start, N_BACK), :] = o
                l_scr[g, rows(start, N_BACK), :] = jnp.broadcast_to(lse, (N_BACK, HEAD_DIM))
                return carry

            lax.fori_loop(0, d * nb, blk_body, 0)
        else:
            ii = lax.broadcasted_iota(jnp.int32, (N_BACK, N_BACK), 0)
            jj = lax.broadcasted_iota(jnp.int32, (N_BACK, N_BACK), 1)

            def blk_body(r, carry, g=g, rows=rows, ii=ii, jj=jj):
                q = q_scr[g, rows(r, N_BACK), :].astype(BF16)
                k = k_scrs[g][rows(r, N_BACK), :].astype(BF16)
                v = v_scrs[g][rows(r, N_BACK), :].astype(BF16)
                o, lse = _softmax_block(q, k, v, jj <= ii)
                o_scr[g, rows(r, N_BACK), :] = o
                l_scr[g, rows(r, N_BACK), :] = jnp.broadcast_to(lse, (N_BACK, HEAD_DIM))
                return carry

            lax.fori_loop(0, d, blk_body, 0)

    def merge_body(t, carry):
        r0 = pl.multiple_of(t * tm, tm)
        sl = pl.ds(r0, tm)
        l0, l1, l2 = l_scr[0, sl, :], l_scr[1, sl, :], l_scr[2, sl, :]
        mx = jnp.maximum(jnp.maximum(l0, l1), l2)
        w0, w1, w2 = jnp.exp(l0 - mx), jnp.exp(l1 - mx), jnp.exp(l2 - mx)
        o = (w0 * o_scr[0, sl, :] + w1 * o_scr[1, sl, :] + w2 * o_scr[2, sl, :]) * (1.0 / (w0 + w1 + w2))
        y_ref[0, sl, :] = (o * _silu(gate_scr[sl, :])).astype(BF16)
        return carry

    lax.fori_loop(0, seq // tm, merge_body, 0)


def _attn_mixer(h, w_heads, cos, sin, q_gain, k_gain):
    B, S, D = h.shape
    n_heads = D // HEAD_DIM
    per_head = w_heads.shape[1] // n_heads
    assert S % (N_BACK * DIL_PAIRS[-1][1]) == 0
    pads = tuple(N_BACK * d if S // (N_BACK * d) > 1 else 0 for _, d in DIL_PAIRS)
    kv_scr = [pltpu.VMEM((p + S, HEAD_DIM), F32) for p in pads]
    return pl.pallas_call(
        functools.partial(_attn_kernel, seq=S),
        grid=(B, n_heads),
        in_specs=[
            pl.BlockSpec((1, S, D), lambda b, hd: (b, 0, 0)),
            pl.BlockSpec((D, per_head), lambda b, hd: (0, hd)),
            pl.BlockSpec((1, S, HEAD_DIM), lambda b, hd: (b, 0, 0)),
            pl.BlockSpec((1, S, HEAD_DIM), lambda b, hd: (b, 0, 0)),
            pl.BlockSpec((N_GROUPS, HEAD_DIM), lambda b, hd: (0, 0)),
            pl.BlockSpec((N_GROUPS, HEAD_DIM), lambda b, hd: (0, 0)),
        ],
        out_specs=pl.BlockSpec((1, S, HEAD_DIM), lambda b, hd: (b, 0, hd)),
        out_shape=jax.ShapeDtypeStruct((B, S, D), BF16),
        scratch_shapes=[pltpu.VMEM((N_GROUPS, S, HEAD_DIM), F32)] + kv_scr + kv_scr + [
            pltpu.VMEM((N_GROUPS, S, HEAD_DIM), F32),
            pltpu.VMEM((N_GROUPS, S, HEAD_DIM), F32),
            pltpu.VMEM((S, HEAD_DIM), F32),
        ],
        compiler_params=_params(("parallel", "arbitrary")),
        name="attn_mixer",
    )(h, w_heads, cos, sin, q_gain, k_gain)


def _hgrn_level_masks(c):
    t = np.arange(c)[:, None]
    s = np.arange(c)[None, :]
    masks = [(t == s)]
    m = 1
    while m < c:
        masks.append((t // (2 * m) == s // (2 * m)) & (t % (2 * m) >= m) & (s % (2 * m) < m))
        m *= 2
    return np.stack(masks).astype(np.float32)


def _hgrn_kernel(h_ref, w_ref, la_ref, lc_ref, omlb_ref, og_ref, tri_ref, msk_ref, y_ref,
                 q_scr, lf_scr, k_scr, v_scr, gate_scr, g_scr, *, seq):
    tm = 256
    C = HGRN_CHUNK

    def proj_body(t, carry):
        sl = pl.ds(pl.multiple_of(t * tm, tm), tm)
        ht = h_ref[0, sl, :]
        qf = _dot(ht, w_ref[:, :256])
        vg = _dot(ht, w_ref[:, 256:])
        f = qf[:, HEAD_DIM:]
        q_scr[sl, :] = qf[:, :HEAD_DIM]
        v_scr[sl, :] = vg[:, :HEAD_DIM]
        gate_scr[sl, :] = vg[:, HEAD_DIM:]
        sp = jnp.log1p(jnp.exp(-jnp.abs(f)))
        u = lc_ref[...] + (jnp.minimum(f, 0.0) - sp)
        la = la_ref[...]
        lf_scr[sl, :] = jnp.maximum(la, u) + jnp.log1p(jnp.exp(-jnp.abs(la - u)))
        k_scr[sl, :] = omlb_ref[...] * jnp.exp(jnp.minimum(-f, 0.0) - sp)
        return carry

    lax.fori_loop(0, seq // tm, proj_body, 0)

    row = lax.broadcasted_iota(jnp.int32, (C, HEAD_DIM), 0)
    n_levels = msk_ref.shape[0]

    def chunk_body(ci, st):
        sl = pl.ds(pl.multiple_of(ci * C, C), C)
        q = q_scr[sl, :]
        lf = lf_scr[sl, :]
        k = k_scr[sl, :]
        v = v_scr[sl, :]
        hi = lf.astype(BF16)
        r1 = lf - hi.astype(F32)
        mid = r1.astype(BF16)
        lo = (r1 - mid.astype(F32)).astype(BF16)
        tri = tri_ref[...]
        G = _dot(tri, hi) + _dot(tri, mid) + _dot(tri, lo)
        g_scr[...] = G
        glast = g_scr[C - 1:C, :]

        o = _dot_nt((q * jnp.exp(G)).astype(BF16), st.astype(BF16))

        qb = q.astype(BF16)
        kb = k.astype(BF16)
        f = jnp.exp(lf)
        a = msk_ref[0] * _dot_nt(qb, kb)
        a = a + msk_ref[1] * _dot_nt((q * f).astype(BF16), kb)
        f_prev = pltpu.roll(f, 1, 0)
        f_next = pltpu.roll(f, C - 1, 0)
        q2 = q * f * jnp.where((row & 3) == 3, f_prev, 1.0)
        k2 = k * jnp.where((row & 3) == 0, f_next, 1.0)
        a = a + msk_ref[2] * _dot_nt(q2.astype(BF16), k2.astype(BF16))
        for lvl in range(3, n_levels):
            m = 2 ** (lvl - 1)
            gref = jnp.concatenate(
                [jnp.broadcast_to(g_scr[pl.ds(i * 2 * m + m - 1, 1), :], (2 * m, HEAD_DIM))
                 for i in range(C // (2 * m))], axis=0)
            e = jnp.exp(-jnp.abs(G - gref))
            a = a + msk_ref[lvl] * _dot_nt((q * e).astype(BF16), (k * e).astype(BF16))
        o = o + _dot(a.astype(BF16), v.astype(BF16))

        y = _rms(o) * og_ref[...] * _silu(gate_scr[sl, :])
        y_ref[0, sl, :] = y.astype(BF16)

        kd = k * jnp.exp(glast - G)
        return st * jnp.exp(glast) + _dot(v.T.astype(BF16), kd.astype(BF16))

    lax.fori_loop(0, seq // C, chunk_body, jnp.zeros((HEAD_DIM, HEAD_DIM), F32))


def _hgrn_mixer(h, w_heads, lb, o_gain):
    B, S, D = h.shape
    n_heads = D // HEAD_DIM
    per_head = w_heads.shape[1] // n_heads
    C = HGRN_CHUNK
    tri = jnp.asarray(np.tril(np.ones((C, C), np.float32)), BF16)
    masks = jnp.asarray(_hgrn_level_masks(C))
    row = lambda a: a[None, :]
    lane_spec = pl.BlockSpec((1, HEAD_DIM), lambda b, hd: (0, hd))
    seq_scr = pltpu.VMEM((S, HEAD_DIM), F32)
    return pl.pallas_call(
        functools.partial(_hgrn_kernel, seq=S),
        grid=(B, n_heads),
        in_specs=[
            pl.BlockSpec((1, S, D), lambda b, hd: (b, 0, 0)),
            pl.BlockSpec((D, per_head), lambda b, hd: (0, hd)),
            lane_spec, lane_spec, lane_spec,
            pl.BlockSpec((1, HEAD_DIM), lambda b, hd: (0, 0)),
            pl.BlockSpec((C, C), lambda b, hd: (0, 0)),
            pl.BlockSpec(masks.shape, lambda b, hd: (0, 0, 0)),
        ],
        out_specs=pl.BlockSpec((1, S, HEAD_DIM), lambda b, hd: (b, 0, hd)),
        out_shape=jax.ShapeDtypeStruct((B, S, D), BF16),
        scratch_shapes=[seq_scr] * 5 + [pltpu.VMEM((C, HEAD_DIM), F32)],
        compiler_params=_params(("parallel", "arbitrary")),
        name="hgrn_mixer",
    )(h, w_heads, row(jnp.log(lb)), row(jnp.log1p(-lb)), row(1.0 - lb), row(o_gain), tri, masks)


def _out_kernel(x_ref, ym_ref, ye_ref, w_ref, g_ref, xo_ref, *h_ref, mix_width):
    acc = x_ref[...] + _dot(ym_ref[...], w_ref[:mix_width, :]) + _dot(ye_ref[...], w_ref[mix_width:, :])
    xo_ref[...] = acc
    if h_ref:
        h_ref[0][...] = (_rms(acc) * g_ref[...]).astype(BF16)


def _out_proj(x, y_mix, y_mem, w_out, next_gain):
    N, D = x.shape
    tm = 512
    wm, we = y_mix.shape[1], y_mem.shape[1]
    want_h = next_gain is not None
    gain = next_gain if want_h else jnp.ones((D,), F32)
    tile = lambda w: pl.BlockSpec((tm, w), lambda i: (i, 0))
    out_specs = [tile(D)] + ([tile(D)] if want_h else [])
    out_shape = [jax.ShapeDtypeStruct((N, D), F32)] + ([jax.ShapeDtypeStruct((N, D), BF16)] if want_h else [])
    res = pl.pallas_call(
        functools.partial(_out_kernel, mix_width=wm),
        grid=(N // tm,),
        in_specs=[tile(D), tile(wm), tile(we),
                  pl.BlockSpec((wm + we, D), lambda i: (0, 0)),
                  pl.BlockSpec((1, D), lambda i: (0, 0))],
        out_specs=out_specs,
        out_shape=out_shape,
        compiler_params=_params(("parallel",)),
        name="out_proj",
    )(x, y_mix, y_mem, w_out.astype(BF16), gain[None, :])
    return (res[0], res[1]) if want_h else (res[0], None)


def _split_in_weights(w, n_mix_parts, n_heads):
    D = w.shape[0]
    wb = w.astype(BF16)
    mix = wb[:, :n_mix_parts * D].reshape(D, n_mix_parts, n_heads, HEAD_DIM)
    q_mem = wb[:, n_mix_parts * D:n_mix_parts * D + MEM_WIDTH]
    gate = wb[:, n_mix_parts * D + MEM_WIDTH:]
    gate_mix = gate[:, :D].reshape(D, 1, n_heads, HEAD_DIM)
    heads = jnp.concatenate([mix, gate_mix], axis=1).transpose(0, 2, 1, 3)
    return heads.reshape(D, n_heads * (n_mix_parts + 1) * HEAD_DIM), jnp.concatenate([q_mem, gate[:, D:]], axis=1)


def kernel(x, mem, positions, norm_gain, w_in_a, q_gain_a, k_gain_a, w_out_a, w_in_b, lb_logits,
           o_gain_b, w_out_b, mem_norm_gain, w_mem_kv, mem_q_gain, mem_k_gain):
    B, S, D = x.shape
    depth = norm_gain.shape[0]
    n_heads = D // HEAD_DIM

    sm = jax.nn.softmax(lb_logits.astype(F32), axis=0)
    lower_bounds = jnp.cumsum(sm, axis=0) - sm[0:1]

    h, cos, sin = _prologue(x, positions, norm_gain[0])
    km, vm = _memkv(mem, mem_norm_gain, w_mem_kv, mem_k_gain)

    xf = x.reshape(B * S, D)
    for l in range(depth):
        j = l // 2
        if l % 2 == 0:
            w_heads, w_mem = _split_in_weights(w_in_a[j], 3 * N_GROUPS, n_heads)
            y_mix = _attn_mixer(h, w_heads, cos, sin, q_gain_a[j], k_gain_a[j])
            w_out = w_out_a[j]
        else:
            w_heads, w_mem = _split_in_weights(w_in_b[j], 3, n_heads)
            y_mix = _hgrn_mixer(h, w_heads, lower_bounds[l], o_gain_b[j])
            w_out = w_out_b[j]
        y_mem = _memattn(h, w_mem, km, vm, l, mem_q_gain[l])
        next_gain = norm_gain[l + 1] if l + 1 < depth else None
        xf, hf = _out_proj(xf, y_mix.reshape(B * S, D), y_mem.reshape(B * S, MEM_WIDTH), w_out, next_gain)
        h = hf.reshape(B, S, D) if hf is not None else None
    return xf.reshape(B, S, D)
```

```python
import functools
import math

import numpy as np
import jax
import jax.numpy as jnp
from jax import lax
from jax.experimental import pallas as pl
from jax.experimental.pallas import tpu as pltpu

F32 = jnp.float32
BF16 = jnp.bfloat16

HEAD_DIM = 128
ROT_DIM = HEAD_DIM // 4
ROT_HALF = ROT_DIM // 2
ROPE_THETA = 500000.0
DIL_PAIRS = ((128, 1), (512, 4), (2048, 16))
N_GROUPS = len(DIL_PAIRS)
N_BACK = 128
MEM_HEADS = 4
MEM_WIDTH = MEM_HEADS * HEAD_DIM
EPS = 1e-6
ATTN_SCALE = 1.0 / math.sqrt(HEAD_DIM)
NEG_BIG = -1e30
HGRN_CHUNK = 128
HGRN_UNROLL = 4
HGRN_OUT_UNROLL = 4
LOG2E = 1.4426950408889634
ATTN_UNROLL = 8
VMEM_LIMIT = 52 * 1024 * 1024

_NT = (((1,), (1,)), ((), ()))


def _dot(a, b):
    return jnp.dot(a, b, preferred_element_type=F32)


def _dot_nt(a, b):
    return lax.dot_general(a, b, _NT, preferred_element_type=F32)


def _rms(x):
    return x * lax.rsqrt(jnp.mean(x * x, axis=-1, keepdims=True) + EPS)


def _silu(g):
    return g * (1.0 / (1.0 + jnp.exp(-g)))


def _params(sem):
    return pltpu.CompilerParams(dimension_semantics=sem, vmem_limit_bytes=VMEM_LIMIT)


def _prologue_kernel(x_ref, pos_ref, g_ref, freq_ref, sgn_ref, h_ref, cos_ref, sin_ref):
    h_ref[0] = (_rms(x_ref[0]) * g_ref[...]).astype(BF16)
    ang = pos_ref[0] * freq_ref[...]
    cos_ref[0] = jnp.cos(ang)
    sin_ref[0] = jnp.sin(ang) * sgn_ref[...]


def _prologue(x, positions, gain0):
    B, S, D = x.shape
    ts = 512
    inv_freq = ROPE_THETA ** (-jnp.arange(0, ROT_DIM, 2, dtype=F32) / ROT_DIM)
    freq = jnp.concatenate([inv_freq, inv_freq, jnp.zeros((HEAD_DIM - ROT_DIM,), F32)])[None, :]
    sgn = jnp.asarray(np.where(np.arange(HEAD_DIM) < ROT_HALF, -1.0, 1.0), F32)[None, :]
    pos = positions.astype(F32)[..., None]
    return pl.pallas_call(
        _prologue_kernel,
        grid=(B, S // ts),
        in_specs=[
            pl.BlockSpec((1, ts, D), lambda b, t: (b, t, 0)),
            pl.BlockSpec((1, ts, 1), lambda b, t: (b, t, 0)),
            pl.BlockSpec((1, D), lambda b, t: (0, 0)),
            pl.BlockSpec((1, HEAD_DIM), lambda b, t: (0, 0)),
            pl.BlockSpec((1, HEAD_DIM), lambda b, t: (0, 0)),
        ],
        out_specs=[
            pl.BlockSpec((1, ts, D), lambda b, t: (b, t, 0)),
            pl.BlockSpec((1, ts, HEAD_DIM), lambda b, t: (b, t, 0)),
            pl.BlockSpec((1, ts, HEAD_DIM), lambda b, t: (b, t, 0)),
        ],
        out_shape=[
            jax.ShapeDtypeStruct((B, S, D), BF16),
            jax.ShapeDtypeStruct((B, S, HEAD_DIM), F32),
            jax.ShapeDtypeStruct((B, S, HEAD_DIM), F32),
        ],
        compiler_params=_params(("parallel", "parallel")),
        name="prologue",
    )(x, pos, gain0[None, :], freq, sgn)


def _memkv_kernel(mem_ref, mg_ref, w_ref, kg_ref, km_ref, vm_ref):
    mn = (_rms(mem_ref[0]) * mg_ref[0]).astype(BF16)
    kv = _dot(mn, w_ref[0])
    for hd in range(MEM_HEADS):
        sl = slice(hd * HEAD_DIM, (hd + 1) * HEAD_DIM)
        km_ref[0, 0, :, sl] = (_rms(kv[:, sl]) * kg_ref[0]).astype(BF16)
    vm_ref[0, 0] = kv[:, MEM_WIDTH:].astype(BF16)


def _memkv(mem, mem_norm_gain, w_mem_kv, mem_k_gain):
    B, M, D = mem.shape
    depth = w_mem_kv.shape[0]
    out = jax.ShapeDtypeStruct((depth, B, M, MEM_WIDTH), BF16)
    return pl.pallas_call(
        _memkv_kernel,
        grid=(depth, B),
        in_specs=[
            pl.BlockSpec((1, M, D), lambda l, b: (b, 0, 0)),
            pl.BlockSpec((1, 1, D), lambda l, b: (l, 0, 0)),
            pl.BlockSpec((1, D, 2 * MEM_WIDTH), lambda l, b: (l, 0, 0)),
            pl.BlockSpec((1, 1, HEAD_DIM), lambda l, b: (l, 0, 0)),
        ],
        out_specs=[
            pl.BlockSpec((1, 1, M, MEM_WIDTH), lambda l, b: (l, b, 0, 0)),
            pl.BlockSpec((1, 1, M, MEM_WIDTH), lambda l, b: (l, b, 0, 0)),
        ],
        out_shape=[out, out],
        compiler_params=_params(("parallel", "parallel")),
        name="memkv",
    )(mem, mem_norm_gain[:, None, :], w_mem_kv.astype(BF16), mem_k_gain[:, None, :])


def _memattn_kernel(h_ref, w_ref, km_ref, vm_ref, qg_ref, y_ref):
    ht = h_ref[0]
    q_all = _dot(ht, w_ref[:, :MEM_WIDTH])
    gate = _dot(ht, w_ref[:, MEM_WIDTH:])
    for hd in range(MEM_HEADS):
        sl = slice(hd * HEAD_DIM, (hd + 1) * HEAD_DIM)
        qn = (_rms(q_all[:, sl]) * (qg_ref[...] * ATTN_SCALE)).astype(BF16)
        s = _dot_nt(qn, km_ref[0, 0, :, sl])
        m = jnp.max(s, axis=-1, keepdims=True)
        p = jnp.exp(s - m)
        den = jnp.sum(p, axis=-1, keepdims=True)
        o = _dot(p.astype(BF16), vm_ref[0, 0, :, sl]) * (1.0 / den)
        y_ref[0, :, sl] = (o * _silu(gate[:, sl])).astype(BF16)


def _memattn(h, w_mem, km, vm, layer, q_gain):
    B, S, D = h.shape
    M = km.shape[2]
    ts = 512
    return pl.pallas_call(
        _memattn_kernel,
        grid=(B, S // ts),
        in_specs=[
            pl.BlockSpec((1, ts, D), lambda b, t: (b, t, 0)),
            pl.BlockSpec((D, 2 * MEM_WIDTH), lambda b, t: (0, 0)),
            pl.BlockSpec((1, 1, M, MEM_WIDTH), lambda b, t: (layer, b, 0, 0)),
            pl.BlockSpec((1, 1, M, MEM_WIDTH), lambda b, t: (layer, b, 0, 0)),
            pl.BlockSpec((1, HEAD_DIM), lambda b, t: (0, 0)),
        ],
        out_specs=pl.BlockSpec((1, ts, MEM_WIDTH), lambda b, t: (b, t, 0)),
        out_shape=jax.ShapeDtypeStruct((B, S, MEM_WIDTH), BF16),
        compiler_params=_params(("parallel", "parallel")),
        name="memattn",
    )(h, w_mem, km, vm, q_gain[None, :])


def _rotary(x, cos, sin):
    lane = lax.broadcasted_iota(jnp.int32, x.shape, 1)
    swapped = jnp.where(lane < ROT_HALF,
                        pltpu.roll(x, HEAD_DIM - ROT_HALF, 1),
                        pltpu.roll(x, ROT_HALF, 1))
    return x * cos + swapped * sin


def _softmax_block(q, k, v, valid):
    s = jnp.where(valid, _dot_nt(q, k), NEG_BIG)
    m = jnp.max(s, axis=-1, keepdims=True)
    p = jnp.exp(s - m)
    den = jnp.sum(p, axis=-1, keepdims=True)
    o = _dot(p.astype(BF16), v) * (1.0 / den)
    return o, m + jnp.log(den)


def _attn_kernel(h_ref, w_ref, cos_ref, sin_ref, qg_ref, kg_ref, y_ref,
                 q_scr, k0_scr, k1_scr, k2_scr, v0_scr, v1_scr, v2_scr,
                 o_scr, l_scr, gate_scr, *, seq):
    k_scrs = (k0_scr, k1_scr, k2_scr)
    v_scrs = (v0_scr, v1_scr, v2_scr)
    pads = tuple(N_BACK * d if seq // (N_BACK * d) > 1 else 0 for _, d in DIL_PAIRS)
    tm = 256

    for g in range(N_GROUPS):
        if pads[g]:
            k_scrs[g][pl.ds(0, pads[g]), :] = jnp.zeros((pads[g], HEAD_DIM), F32)
            v_scrs[g][pl.ds(0, pads[g]), :] = jnp.zeros((pads[g], HEAD_DIM), F32)

    def proj_body(t, carry):
        r0 = pl.multiple_of(t * tm, tm)
        ht = h_ref[0, pl.ds(r0, tm), :]
        cos = cos_ref[0, pl.ds(r0, tm), :]
        sin = sin_ref[0, pl.ds(r0, tm), :]
        for pair in range(5):
            cols = _dot(ht, w_ref[:, pair * 256:(pair + 1) * 256])
            for half in range(2):
                blk = 2 * pair + half
                c = cols[:, half * HEAD_DIM:(half + 1) * HEAD_DIM]
                if blk < 3:
                    g = blk
                    qn = _rms(c) * (qg_ref[g:g + 1, :] * ATTN_SCALE)
                    q_scr[g, pl.ds(r0, tm), :] = _rotary(qn, cos, sin)
                elif blk < 6:
                    g = blk - 3
                    kn = _rms(c) * kg_ref[g:g + 1, :]
                    k_scrs[g][pl.ds(pads[g] + r0, tm), :] = _rotary(kn, cos, sin)
                elif blk < 9:
                    g = blk - 6
                    v_scrs[g][pl.ds(pads[g] + r0, tm), :] = c
                else:
                    gate_scr[pl.ds(r0, tm), :] = c
        return carry

    lax.fori_loop(0, seq // tm, proj_body, 0)

    for g, (_, d) in enumerate(DIL_PAIRS):
        nb = seq // (N_BACK * d)
        nk = (2 if nb > 1 else 1) * N_BACK
        ii = lax.broadcasted_iota(jnp.int32, (N_BACK, nk), 0)
        jj = lax.broadcasted_iota(jnp.int32, (N_BACK, nk), 1)
        band = (jj >= ii) & (jj <= ii + N_BACK) if nb > 1 else jj <= ii

        def rows(start, n, d=d):
            return pl.ds(start, n) if d == 1 else pl.ds(start, n, stride=d)

        def blk_body(it, carry, g=g, d=d, nb=nb, nk=nk, rows=rows, jj=jj, band=band):
            starts, scores, outs = [], [], []
            for u in range(ATTN_UNROLL):
                idx = it * ATTN_UNROLL + u
                r = idx // nb
                c = idx % nb
                start = c * (N_BACK * d) + r
                q = q_scr[g, rows(start, N_BACK), :].astype(BF16)
                k = k_scrs[g][rows(start, nk), :].astype(BF16)
                valid = band & (jj >= jnp.where(c == 0, N_BACK, 0)) if nb > 1 else band
                starts.append(start)
                scores.append(jnp.where(valid, _dot_nt(q, k), NEG_BIG))
            for start, s in zip(starts, scores):
                m = jnp.max(s, axis=-1, keepdims=True)
                p = jnp.exp(s - m)
                den = jnp.sum(p, axis=-1, keepdims=True)
                v = v_scrs[g][rows(start, nk), :].astype(BF16)
                outs.append((_dot(p.astype(BF16), v), den, m))
            for start, (o, den, m) in zip(starts, outs):
                o_scr[g, rows(start, N_BACK), :] = o * (1.0 / den)
                l_scr[g, rows(start, N_BACK), :] = jnp.broadcast_to(m + jnp.log(den), (N_BACK, HEAD_DIM))
            return carry

        lax.fori_loop(0, d * nb // ATTN_UNROLL, blk_body, 0)

    def merge_body(t, carry):
        r0 = pl.multiple_of(t * tm, tm)
        sl = pl.ds(r0, tm)
        l0, l1, l2 = l_scr[0, sl, :], l_scr[1, sl, :], l_scr[2, sl, :]
        mx = jnp.maximum(jnp.maximum(l0, l1), l2)
        w0, w1, w2 = jnp.exp(l0 - mx), jnp.exp(l1 - mx), jnp.exp(l2 - mx)
        o = (w0 * o_scr[0, sl, :] + w1 * o_scr[1, sl, :] + w2 * o_scr[2, sl, :]) * (1.0 / (w0 + w1 + w2))
        y_ref[0, sl, :] = (o * _silu(gate_scr[sl, :])).astype(BF16)
        return carry

    lax.fori_loop(0, seq // tm, merge_body, 0)


def _attn_mixer(h, w_heads, cos, sin, q_gain, k_gain):
    B, S, D = h.shape
    n_heads = D // HEAD_DIM
    per_head = w_heads.shape[1] // n_heads
    assert S % (N_BACK * DIL_PAIRS[-1][1]) == 0
    pads = tuple(N_BACK * d if S // (N_BACK * d) > 1 else 0 for _, d in DIL_PAIRS)
    kv_scr = [pltpu.VMEM((p + S, HEAD_DIM), F32) for p in pads]
    return pl.pallas_call(
        functools.partial(_attn_kernel, seq=S),
        grid=(B, n_heads),
        in_specs=[
            pl.BlockSpec((1, S, D), lambda b, hd: (b, 0, 0)),
            pl.BlockSpec((D, per_head), lambda b, hd: (0, hd)),
            pl.BlockSpec((1, S, HEAD_DIM), lambda b, hd: (b, 0, 0)),
            pl.BlockSpec((1, S, HEAD_DIM), lambda b, hd: (b, 0, 0)),
            pl.BlockSpec((N_GROUPS, HEAD_DIM), lambda b, hd: (0, 0)),
            pl.BlockSpec((N_GROUPS, HEAD_DIM), lambda b, hd: (0, 0)),
        ],
        out_specs=pl.BlockSpec((1, S, HEAD_DIM), lambda b, hd: (b, 0, hd)),
        out_shape=jax.ShapeDtypeStruct((B, S, D), BF16),
        scratch_shapes=[pltpu.VMEM((N_GROUPS, S, HEAD_DIM), F32)] + kv_scr + kv_scr + [
            pltpu.VMEM((N_GROUPS, S, HEAD_DIM), F32),
            pltpu.VMEM((N_GROUPS, S, HEAD_DIM), F32),
            pltpu.VMEM((S, HEAD_DIM), F32),
        ],
        compiler_params=_params(("parallel", "arbitrary")),
        name="attn_mixer",
    )(h, w_heads, cos, sin, q_gain, k_gain)


def _hgrn_level_masks(c):
    t = np.arange(c)[:, None]
    s = np.arange(c)[None, :]
    masks = [(t == s)]
    m = 1
    while m < c:
        masks.append((t // (2 * m) == s // (2 * m)) & (t % (2 * m) >= m) & (s % (2 * m) < m))
        m *= 2
    return np.stack(masks).astype(np.float32)


def _neg_abs(x):
    return pltpu.bitcast(pltpu.bitcast(x, jnp.uint32) | jnp.uint32(0x80000000), F32)


def _hgrn_kernel(h_ref, w_ref, la_ref, lc_ref, omlb_ref, og_ref, tri_ref, msk_ref, y_ref,
                 q_scr, lf_scr, k_scr, v_scr, gate_scr, g_scr, oi_scr, qg_scr, ds_scr,
                 eg_scr, st_scr, *, seq):
    tm = 256
    C = HGRN_CHUNK
    n_chunks = seq // C

    def proj_body(t, carry):
        sl = pl.ds(pl.multiple_of(t * tm, tm), tm)
        ht = h_ref[0, sl, :]
        qf = _dot(ht, w_ref[:, :256])
        vg = _dot(ht, w_ref[:, 256:])
        f = qf[:, HEAD_DIM:]
        q_scr[sl, :] = qf[:, :HEAD_DIM]
        v_scr[sl, :] = vg[:, :HEAD_DIM]
        gate_scr[sl, :] = vg[:, HEAD_DIM:]
        sp = jnp.log1p(jnp.exp(-jnp.abs(f)))
        u = lc_ref[...] + (jnp.minimum(f, 0.0) - sp)
        la = la_ref[...]
        lf = jnp.maximum(la, u) + jnp.log1p(jnp.exp(-jnp.abs(la - u)))
        lf_scr[sl, :] = lf * LOG2E
        k_scr[sl, :] = omlb_ref[...] * jnp.exp(jnp.minimum(-f, 0.0) - sp)
        return carry

    lax.fori_loop(0, seq // tm, proj_body, 0)

    row = lax.broadcasted_iota(jnp.int32, (C, HEAD_DIM), 0)
    n_levels = msk_ref.shape[0]

    def intra_body(it, carry):
        chunks = [it * HGRN_UNROLL + u for u in range(HGRN_UNROLL)]
        sls = [pl.ds(pl.multiple_of(c * C, C), C) for c in chunks]
        tri = tri_ref[...]
        gs = []
        for sl in sls:
            lf = lf_scr[sl, :]
            hi = lf.astype(BF16)
            r1 = lf - hi.astype(F32)
            mid = r1.astype(BF16)
            lo = (r1 - mid.astype(F32)).astype(BF16)
            gs.append(_dot(tri, hi) + _dot(tri, mid) + _dot(tri, lo))
        scores, kds = [], []
        for u, (c, sl, G) in enumerate(zip(chunks, sls, gs)):
            g_scr[u] = G
            q = q_scr[sl, :]
            k = k_scr[sl, :]
            f = jnp.exp2(lf_scr[sl, :])
            a = msk_ref[0] * _dot_nt(q.astype(BF16), k.astype(BF16))
            x = jnp.where((row & 1) == 1, q * f, k).astype(BF16)
            a = a + msk_ref[1] * _dot_nt(x, x)
            r4 = row & 3
            dec = jnp.where(r4 == 3, f * pltpu.roll(f, 1, 0),
                            jnp.where(r4 == 2, f, jnp.where(r4 == 0, pltpu.roll(f, C - 1, 0), 1.0)))
            x = (jnp.where(r4 >= 2, q, k) * dec).astype(BF16)
            a = a + msk_ref[2] * _dot_nt(x, x)
            for lvl in range(3, n_levels):
                m = 2 ** (lvl - 1)
                gref = jnp.concatenate(
                    [jnp.broadcast_to(g_scr[u, pl.ds(i * 2 * m + m - 1, 1), :], (2 * m, HEAD_DIM))
                     for i in range(C // (2 * m))], axis=0)
                if m >= 8:
                    qk = jnp.concatenate(
                        [part for i in range(C // (2 * m))
                         for part in (k[i * 2 * m:i * 2 * m + m], q[i * 2 * m + m:(i + 1) * 2 * m])], axis=0)
                else:
                    qk = jnp.where((row & m) != 0, q, k)
                x = (qk * jnp.exp2(_neg_abs(G - gref))).astype(BF16)
                a = a + msk_ref[lvl] * _dot_nt(x, x)
            scores.append(a)
            glast = g_scr[u, C - 1:C, :]
            qg_scr[sl, :] = (q * jnp.exp2(G)).astype(BF16)
            kds.append((k * jnp.exp2(glast - G)).astype(BF16))
            eg_scr[c] = jnp.exp2(glast)
        for c, sl, a, kd in zip(chunks, sls, scores, kds):
            v = v_scr[sl, :]
            oi_scr[sl, :] = _dot(a.astype(BF16), v.astype(BF16))
            ds_scr[c] = _dot(v.T.astype(BF16), kd)
        return carry

    lax.fori_loop(0, n_chunks // HGRN_UNROLL, intra_body, 0)

    def scan_body(c, st):
        st_scr[c] = st.astype(BF16)
        return st * eg_scr[c] + ds_scr[c]

    lax.fori_loop(0, n_chunks, scan_body, jnp.zeros((HEAD_DIM, HEAD_DIM), F32))

    def out_body(it, carry):
        chunks = [it * HGRN_OUT_UNROLL + u for u in range(HGRN_OUT_UNROLL)]
        sls = [pl.ds(pl.multiple_of(c * C, C), C) for c in chunks]
        outs = [oi_scr[sl, :] + _dot_nt(qg_scr[sl, :], st_scr[c]) for c, sl in zip(chunks, sls)]
        for sl, o in zip(sls, outs):
            y = _rms(o) * og_ref[...] * _silu(gate_scr[sl, :])
            y_ref[0, sl, :] = y.astype(BF16)
        return carry

    lax.fori_loop(0, n_chunks // HGRN_OUT_UNROLL, out_body, 0)


def _hgrn_mixer(h, w_heads, lb, o_gain):
    B, S, D = h.shape
    n_heads = D // HEAD_DIM
    per_head = w_heads.shape[1] // n_heads
    C = HGRN_CHUNK
    tri = jnp.asarray(np.tril(np.ones((C, C), np.float32)), BF16)
    masks = jnp.asarray(_hgrn_level_masks(C))
    row = lambda a: a[None, :]
    lane_spec = pl.BlockSpec((1, HEAD_DIM), lambda b, hd: (0, hd))
    seq_scr = pltpu.VMEM((S, HEAD_DIM), F32)
    return pl.pallas_call(
        functools.partial(_hgrn_kernel, seq=S),
        grid=(B, n_heads),
        in_specs=[
            pl.BlockSpec((1, S, D), lambda b, hd: (b, 0, 0)),
            pl.BlockSpec((D, per_head), lambda b, hd: (0, hd)),
            lane_spec, lane_spec, lane_spec,
            pl.BlockSpec((1, HEAD_DIM), lambda b, hd: (0, 0)),
            pl.BlockSpec((C, C), lambda b, hd: (0, 0)),
            pl.BlockSpec(masks.shape, lambda b, hd: (0, 0, 0)),
        ],
        out_specs=pl.BlockSpec((1, S, HEAD_DIM), lambda b, hd: (b, 0, hd)),
        out_shape=jax.ShapeDtypeStruct((B, S, D), BF16),
        scratch_shapes=[seq_scr] * 5 + [
            pltpu.VMEM((HGRN_UNROLL, C, HEAD_DIM), F32),
            seq_scr,
            pltpu.VMEM((S, HEAD_DIM), BF16),
            pltpu.VMEM((S // C, HEAD_DIM, HEAD_DIM), F32),
            pltpu.VMEM((S // C, 1, HEAD_DIM), F32),
            pltpu.VMEM((S // C, HEAD_DIM, HEAD_DIM), BF16),
        ],
        compiler_params=_params(("parallel", "arbitrary")),
        name="hgrn_mixer",
    )(h, w_heads, row(jnp.log(lb)), row(jnp.log1p(-lb)), row(1.0 - lb), row(o_gain), tri, masks)


def _out_kernel(x_ref, ym_ref, ye_ref, w_ref, g_ref, xo_ref, *h_ref, mix_width):
    acc = x_ref[...] + _dot(ym_ref[...], w_ref[:mix_width, :]) + _dot(ye_ref[...], w_ref[mix_width:, :])
    xo_ref[...] = acc
    if h_ref:
        h_ref[0][...] = (_rms(acc) * g_ref[...]).astype(BF16)


def _out_proj(x, y_mix, y_mem, w_out, next_gain):
    N, D = x.shape
    tm = 512
    wm, we = y_mix.shape[1], y_mem.shape[1]
    want_h = next_gain is not None
    gain = next_gain if want_h else jnp.ones((D,), F32)
    tile = lambda w: pl.BlockSpec((tm, w), lambda i: (i, 0))
    out_specs = [tile(D)] + ([tile(D)] if want_h else [])
    out_shape = [jax.ShapeDtypeStruct((N, D), F32)] + ([jax.ShapeDtypeStruct((N, D), BF16)] if want_h else [])
    res = pl.pallas_call(
        functools.partial(_out_kernel, mix_width=wm),
        grid=(N // tm,),
        in_specs=[tile(D), tile(wm), tile(we),
                  pl.BlockSpec((wm + we, D), lambda i: (0, 0)),
                  pl.BlockSpec((1, D), lambda i: (0, 0))],
        out_specs=out_specs,
        out_shape=out_shape,
        compiler_params=_params(("parallel",)),
        name="out_proj",
    )(x, y_mix, y_mem, w_out.astype(BF16), gain[None, :])
    return (res[0], res[1]) if want_h else (res[0], None)


def _split_in_weights(w, n_mix_parts, n_heads):
    D = w.shape[0]
    wb = w.astype(BF16)
    mix = wb[:, :n_mix_parts * D].reshape(D, n_mix_parts, n_heads, HEAD_DIM)
    q_mem = wb[:, n_mix_parts * D:n_mix_parts * D + MEM_WIDTH]
    gate = wb[:, n_mix_parts * D + MEM_WIDTH:]
    gate_mix = gate[:, :D].reshape(D, 1, n_heads, HEAD_DIM)
    heads = jnp.concatenate([mix, gate_mix], axis=1).transpose(0, 2, 1, 3)
    return heads.reshape(D, n_heads * (n_mix_parts + 1) * HEAD_DIM), jnp.concatenate([q_mem, gate[:, D:]], axis=1)


def kernel(x, mem, positions, norm_gain, w_in_a, q_gain_a, k_gain_a, w_out_a, w_in_b, lb_logits,
           o_gain_b, w_out_b, mem_norm_gain, w_mem_kv, mem_q_gain, mem_k_gain):
    B, S, D = x.shape
    depth = norm_gain.shape[0]
    n_heads = D // HEAD_DIM

    sm = jax.nn.softmax(lb_logits.astype(F32), axis=0)
    lower_bounds = jnp.cumsum(sm, axis=0) - sm[0:1]

    h, cos, sin = _prologue(x, positions, norm_gain[0])
    km, vm = _memkv(mem, mem_norm_gain, w_mem_kv, mem_k_gain)

    xf = x.reshape(B * S, D)
    for l in range(depth):
        j = l // 2
        if l % 2 == 0:
            w_heads, w_mem = _split_in_weights(w_in_a[j], 3 * N_GROUPS, n_heads)
            y_mix = _attn_mixer(h, w_heads, cos, sin, q_gain_a[j], k_gain_a[j])
            w_out = w_out_a[j]
        else:
            w_heads, w_mem = _split_in_weights(w_in_b[j], 3, n_heads)
            y_mix = _hgrn_mixer(h, w_heads, lower_bounds[l], o_gain_b[j])
            w_out = w_out_b[j]
        y_mem = _memattn(h, w_mem, km, vm, l, mem_q_gain[l])
        next_gain = norm_gain[l + 1] if l + 1 < depth else None
        xf, hf = _out_proj(xf, y_mix.reshape(B * S, D), y_mem.reshape(B * S, MEM_WIDTH), w_out, next_gain)
        h = hf.reshape(B, S, D) if hf is not None else None
    return xf.reshape(B, S, D)
```

```python
import functools
import math

import numpy as np
import jax
import jax.numpy as jnp
from jax import lax
from jax.experimental import pallas as pl
from jax.experimental.pallas import tpu as pltpu

F32 = jnp.float32
BF16 = jnp.bfloat16

HEAD_DIM = 128
ROT_DIM = HEAD_DIM // 4
ROT_HALF = ROT_DIM // 2
ROPE_THETA = 500000.0
DIL_PAIRS = ((128, 1), (512, 4), (2048, 16))
N_GROUPS = len(DIL_PAIRS)
N_BACK = 128
MEM_HEADS = 4
MEM_WIDTH = MEM_HEADS * HEAD_DIM
EPS = 1e-6
ATTN_SCALE = 1.0 / math.sqrt(HEAD_DIM)
NEG_BIG = -1e30
HGRN_CHUNK = 128
HGRN_UNROLL = 4
HGRN_PROJ_TILES = 2
HGRN_OUT_UNROLL = 4
LOG2E = 1.4426950408889634
ATTN_UNROLL = 8
ATTN_PROJ_TILES = 2
VMEM_LIMIT = 52 * 1024 * 1024

_NT = (((1,), (1,)), ((), ()))


def _dot(a, b):
    return jnp.dot(a, b, preferred_element_type=F32)


def _dot_nt(a, b):
    return lax.dot_general(a, b, _NT, preferred_element_type=F32)


def _rms(x):
    return x * lax.rsqrt(jnp.mean(x * x, axis=-1, keepdims=True) + EPS)


def _silu(g):
    return g * (1.0 / (1.0 + jnp.exp(-g)))


def _params(sem):
    return pltpu.CompilerParams(dimension_semantics=sem, vmem_limit_bytes=VMEM_LIMIT)


def _prologue_kernel(x_ref, pos_ref, g_ref, freq_ref, sgn_ref, h_ref, cos_ref, sin_ref):
    h_ref[0] = (_rms(x_ref[0]) * g_ref[...]).astype(BF16)
    ang = pos_ref[0] * freq_ref[...]
    cos_ref[0] = jnp.cos(ang)
    sin_ref[0] = jnp.sin(ang) * sgn_ref[...]


def _prologue(x, positions, gain0):
    B, S, D = x.shape
    ts = 512
    inv_freq = ROPE_THETA ** (-jnp.arange(0, ROT_DIM, 2, dtype=F32) / ROT_DIM)
    freq = _pair_lanes(jnp.concatenate([inv_freq, inv_freq, jnp.zeros((HEAD_DIM - ROT_DIM,), F32)]))[None, :]
    sgn = jnp.asarray(np.where(np.arange(HEAD_DIM) < ROT_HALF, -1.0, 1.0), F32)[None, :]
    pos = positions.astype(F32)[..., None]
    return pl.pallas_call(
        _prologue_kernel,
        grid=(B, S // ts),
        in_specs=[
            pl.BlockSpec((1, ts, D), lambda b, t: (b, t, 0)),
            pl.BlockSpec((1, ts, 1), lambda b, t: (b, t, 0)),
            pl.BlockSpec((1, D), lambda b, t: (0, 0)),
            pl.BlockSpec((1, HEAD_DIM), lambda b, t: (0, 0)),
            pl.BlockSpec((1, HEAD_DIM), lambda b, t: (0, 0)),
        ],
        out_specs=[
            pl.BlockSpec((1, ts, D), lambda b, t: (b, t, 0)),
            pl.BlockSpec((1, ts, HEAD_DIM), lambda b, t: (b, t, 0)),
            pl.BlockSpec((1, ts, HEAD_DIM), lambda b, t: (b, t, 0)),
        ],
        out_shape=[
            jax.ShapeDtypeStruct((B, S, D), BF16),
            jax.ShapeDtypeStruct((B, S, HEAD_DIM), F32),
            jax.ShapeDtypeStruct((B, S, HEAD_DIM), F32),
        ],
        compiler_params=_params(("parallel", "parallel")),
        name="prologue",
    )(x, pos, gain0[None, :], freq, sgn)


def _memkv_kernel(mem_ref, mg_ref, w_ref, kg_ref, km_ref, vm_ref):
    mn = (_rms(mem_ref[0]) * mg_ref[0]).astype(BF16)
    kv = _dot(mn, w_ref[0])
    for hd in range(MEM_HEADS):
        sl = slice(hd * HEAD_DIM, (hd + 1) * HEAD_DIM)
        km_ref[0, 0, :, sl] = (_rms(kv[:, sl]) * kg_ref[0]).astype(BF16)
    vm_ref[0, 0] = kv[:, MEM_WIDTH:].astype(BF16)


def _memkv(mem, mem_norm_gain, w_mem_kv, mem_k_gain):
    B, M, D = mem.shape
    depth = w_mem_kv.shape[0]
    out = jax.ShapeDtypeStruct((depth, B, M, MEM_WIDTH), BF16)
    return pl.pallas_call(
        _memkv_kernel,
        grid=(depth, B),
        in_specs=[
            pl.BlockSpec((1, M, D), lambda l, b: (b, 0, 0)),
            pl.BlockSpec((1, 1, D), lambda l, b: (l, 0, 0)),
            pl.BlockSpec((1, D, 2 * MEM_WIDTH), lambda l, b: (l, 0, 0)),
            pl.BlockSpec((1, 1, HEAD_DIM), lambda l, b: (l, 0, 0)),
        ],
        out_specs=[
            pl.BlockSpec((1, 1, M, MEM_WIDTH), lambda l, b: (l, b, 0, 0)),
            pl.BlockSpec((1, 1, M, MEM_WIDTH), lambda l, b: (l, b, 0, 0)),
        ],
        out_shape=[out, out],
        compiler_params=_params(("parallel", "parallel")),
        name="memkv",
    )(mem, mem_norm_gain[:, None, :], w_mem_kv.astype(BF16), mem_k_gain[:, None, :])


def _memattn_kernel(h_ref, w_ref, km_ref, vm_ref, qg_ref, y_ref):
    ht = h_ref[0]
    q_all = _dot(ht, w_ref[:, :MEM_WIDTH])
    gate = _dot(ht, w_ref[:, MEM_WIDTH:])
    for hd in range(MEM_HEADS):
        sl = slice(hd * HEAD_DIM, (hd + 1) * HEAD_DIM)
        qn = (_rms(q_all[:, sl]) * (qg_ref[...] * ATTN_SCALE)).astype(BF16)
        s = _dot_nt(qn, km_ref[0, 0, :, sl])
        m = jnp.max(s, axis=-1, keepdims=True)
        p = jnp.exp(s - m)
        den = jnp.sum(p, axis=-1, keepdims=True)
        o = _dot(p.astype(BF16), vm_ref[0, 0, :, sl]) * (1.0 / den)
        y_ref[0, :, sl] = (o * _silu(gate[:, sl])).astype(BF16)


def _memattn(h, w_mem, km, vm, layer, q_gain):
    B, S, D = h.shape
    M = km.shape[2]
    ts = 512
    return pl.pallas_call(
        _memattn_kernel,
        grid=(B, S // ts),
        in_specs=[
            pl.BlockSpec((1, ts, D), lambda b, t: (b, t, 0)),
            pl.BlockSpec((D, 2 * MEM_WIDTH), lambda b, t: (0, 0)),
            pl.BlockSpec((1, 1, M, MEM_WIDTH), lambda b, t: (layer, b, 0, 0)),
            pl.BlockSpec((1, 1, M, MEM_WIDTH), lambda b, t: (layer, b, 0, 0)),
            pl.BlockSpec((1, HEAD_DIM), lambda b, t: (0, 0)),
        ],
        out_specs=pl.BlockSpec((1, ts, MEM_WIDTH), lambda b, t: (b, t, 0)),
        out_shape=jax.ShapeDtypeStruct((B, S, MEM_WIDTH), BF16),
        compiler_params=_params(("parallel", "parallel")),
        name="memattn",
    )(h, w_mem, km, vm, q_gain[None, :])


def _pair_lanes(x):
    half = HEAD_DIM // 2
    return jnp.concatenate([x[..., :ROT_HALF], x[..., ROT_DIM:half + ROT_HALF],
                            x[..., ROT_HALF:ROT_DIM], x[..., half + ROT_HALF:]], axis=-1)


def _rotary(x, cos, sin):
    return x * cos + pltpu.roll(x, HEAD_DIM // 2, 1) * sin


def _attn_bias():
    i = np.arange(N_BACK)[:, None]
    j = np.arange(2 * N_BACK)[None, :]
    band = (j >= i) & (j <= i + N_BACK)
    masks = np.stack([band, band & (j >= N_BACK), j <= i])
    return np.where(masks, 0.0, NEG_BIG).astype(np.float32)


def _attn_kernel(h_ref, w_ref, cos_ref, sin_ref, qg_ref, kg_ref, bias_ref, y_ref,
                 q0_scr, q1_scr, q2_scr, k0_scr, k1_scr, k2_scr, v0_scr, v1_scr, v2_scr,
                 o_scr, m_scr, den_scr, gate_scr, *, seq):
    q_scrs = (q0_scr, q1_scr, q2_scr)
    k_scrs = (k0_scr, k1_scr, k2_scr)
    v_scrs = (v0_scr, v1_scr, v2_scr)
    pads = tuple(N_BACK * d if seq // (N_BACK * d) > 1 else 0 for _, d in DIL_PAIRS)
    tm = 256

    for g in range(N_GROUPS):
        if pads[g]:
            k_scrs[g][pl.ds(0, pads[g]), :] = jnp.zeros((pads[g], HEAD_DIM), k_scrs[g].dtype)
            v_scrs[g][pl.ds(0, pads[g]), :] = jnp.zeros((pads[g], HEAD_DIM), v_scrs[g].dtype)

    def proj_body(t, carry):
        r0s = [pl.multiple_of((t * ATTN_PROJ_TILES + i) * tm, tm) for i in range(ATTN_PROJ_TILES)]
        hts = [h_ref[0, pl.ds(r0, tm), :] for r0 in r0s]
        for pair in range(5):
            w_pair = w_ref[:, pair * 256:(pair + 1) * 256]
            for r0, ht in zip(r0s, hts):
                cols = _dot(ht, w_pair)
                cos = cos_ref[0, pl.ds(r0, tm), :]
                sin = sin_ref[0, pl.ds(r0, tm), :]
                for half in range(2):
                    blk = 2 * pair + half
                    c = cols[:, half * HEAD_DIM:(half + 1) * HEAD_DIM]
                    if blk < 3:
                        g = blk
                        qn = _rms(c) * (qg_ref[g:g + 1, :] * (ATTN_SCALE * LOG2E))
                        q_scrs[g][pl.ds(r0, tm), :] = _rotary(qn, cos, sin).astype(q_scrs[g].dtype)
                    elif blk < 6:
                        g = blk - 3
                        kn = _rms(c) * kg_ref[g:g + 1, :]
                        k_scrs[g][pl.ds(pads[g] + r0, tm), :] = _rotary(kn, cos, sin).astype(k_scrs[g].dtype)
                    elif blk < 9:
                        g = blk - 6
                        v_scrs[g][pl.ds(pads[g] + r0, tm), :] = c.astype(v_scrs[g].dtype)
                    else:
                        gate_scr[pl.ds(r0, tm), :] = c
        return carry

    lax.fori_loop(0, seq // (tm * ATTN_PROJ_TILES), proj_body, 0)

    ones = jnp.ones((2 * N_BACK, HEAD_DIM), BF16)
    for g, (_, d) in enumerate(DIL_PAIRS):
        nb = seq // (N_BACK * d)
        nk = (2 if nb > 1 else 1) * N_BACK

        def rows(start, n, d=d):
            return pl.ds(start, n) if d == 1 else pl.ds(start, n, stride=d)

        for it in range(d * nb // ATTN_UNROLL):
            scores, outs = [], []
            for u in range(ATTN_UNROLL):
                r, c = divmod(it * ATTN_UNROLL + u, nb)
                start = c * (N_BACK * d) + r
                q = q_scrs[g][rows(start, N_BACK), :].astype(BF16)
                k = k_scrs[g][rows(start, nk), :].astype(BF16)
                if nb == 1:
                    bias = bias_ref[2, :, :N_BACK]
                else:
                    bias = bias_ref[1 if c == 0 else 0]
                scores.append((start, _dot_nt(q, k) + bias))
            for start, s in scores:
                m = jnp.max(s, axis=-1, keepdims=True)
                p = jnp.exp2(s - m).astype(BF16)
                v = v_scrs[g][rows(start, nk), :].astype(BF16)
                outs.append((start, _dot(p, jnp.concatenate([v, ones[:nk]], axis=1)), m))
            for start, od, m in outs:
                o_scr[g, rows(start, N_BACK), :] = od[:, :HEAD_DIM]
                m_scr[g, rows(start, N_BACK), :] = jnp.broadcast_to(m, (N_BACK, HEAD_DIM))
                den_scr[g, rows(start, N_BACK), :] = od[:, HEAD_DIM:]

    def merge_body(t, carry):
        r0 = pl.multiple_of(t * tm, tm)
        sl = pl.ds(r0, tm)
        m0, m1, m2 = m_scr[0, sl, :], m_scr[1, sl, :], m_scr[2, sl, :]
        mx = jnp.maximum(jnp.maximum(m0, m1), m2)
        w0, w1, w2 = jnp.exp2(m0 - mx), jnp.exp2(m1 - mx), jnp.exp2(m2 - mx)
        num = w0 * o_scr[0, sl, :] + w1 * o_scr[1, sl, :] + w2 * o_scr[2, sl, :]
        den = w0 * den_scr[0, sl, :] + w1 * den_scr[1, sl, :] + w2 * den_scr[2, sl, :]
        y_ref[0, sl, :] = (num * (1.0 / den) * _silu(gate_scr[sl, :])).astype(BF16)
        return carry

    lax.fori_loop(0, seq // tm, merge_body, 0)


def _attn_mixer(h, w_heads, cos, sin, q_gain, k_gain):
    B, S, D = h.shape
    n_heads = D // HEAD_DIM
    per_head = w_heads.shape[1] // n_heads
    assert S % (N_BACK * DIL_PAIRS[-1][1]) == 0
    pads = tuple(N_BACK * d if S // (N_BACK * d) > 1 else 0 for _, d in DIL_PAIRS)
    dts = [BF16 if d == 1 else F32 for _, d in DIL_PAIRS]
    q_scr = [pltpu.VMEM((S, HEAD_DIM), dt) for dt in dts]
    kv_scr = [pltpu.VMEM((p + S, HEAD_DIM), dt) for p, dt in zip(pads, dts)]
    grp_scr = pltpu.VMEM((N_GROUPS, S, HEAD_DIM), F32)
    bias = jnp.asarray(_attn_bias())
    return pl.pallas_call(
        functools.partial(_attn_kernel, seq=S),
        grid=(B, n_heads),
        in_specs=[
            pl.BlockSpec((1, S, D), lambda b, hd: (b, 0, 0)),
            pl.BlockSpec((D, per_head), lambda b, hd: (0, hd)),
            pl.BlockSpec((1, S, HEAD_DIM), lambda b, hd: (b, 0, 0)),
            pl.BlockSpec((1, S, HEAD_DIM), lambda b, hd: (b, 0, 0)),
            pl.BlockSpec((N_GROUPS, HEAD_DIM), lambda b, hd: (0, 0)),
            pl.BlockSpec((N_GROUPS, HEAD_DIM), lambda b, hd: (0, 0)),
            pl.BlockSpec(bias.shape, lambda b, hd: (0, 0, 0)),
        ],
        out_specs=pl.BlockSpec((1, S, HEAD_DIM), lambda b, hd: (b, 0, hd)),
        out_shape=jax.ShapeDtypeStruct((B, S, D), BF16),
        scratch_shapes=q_scr + kv_scr + kv_scr + [grp_scr, grp_scr, grp_scr,
                                                  pltpu.VMEM((S, HEAD_DIM), F32)],
        compiler_params=_params(("parallel", "arbitrary")),
        name="attn_mixer",
    )(h, w_heads, cos, sin, q_gain, k_gain, bias)


def _hgrn_level_masks(c):
    t = np.arange(c)[:, None]
    s = np.arange(c)[None, :]
    masks = [(t == s)]
    m = 1
    while m < c:
        masks.append((t // (2 * m) == s // (2 * m)) & (t % (2 * m) >= m) & (s % (2 * m) < m))
        m *= 2
    return np.stack(masks).astype(np.float32)


def _neg_abs(x):
    return pltpu.bitcast(pltpu.bitcast(x, jnp.uint32) | jnp.uint32(0x80000000), F32)


def _hgrn_kernel(h_ref, w_ref, la_ref, lc_ref, omlb_ref, og_ref, tri_ref, msk_ref, y_ref,
                 q_scr, lf_scr, k_scr, v_scr, gate_scr, g_scr, oi_scr, qg_scr, ds_scr,
                 eg_scr, st_scr, *, seq):
    tm = 256
    C = HGRN_CHUNK
    n_chunks = seq // C

    def proj_body(t, carry):
        sls = [pl.ds((t * HGRN_PROJ_TILES + i) * tm, tm) for i in range(HGRN_PROJ_TILES)]
        for sl in sls:
            ht = h_ref[0, sl, :]
            qf = _dot(ht, w_ref[:, :256])
            vg = _dot(ht, w_ref[:, 256:])
            f = qf[:, HEAD_DIM:]
            q_scr[sl, :] = qf[:, :HEAD_DIM]
            v_scr[sl, :] = vg[:, :HEAD_DIM]
            gate_scr[sl, :] = vg[:, HEAD_DIM:]
            sp = jnp.log(1.0 + jnp.exp2(_neg_abs(f) * LOG2E))
            u = lc_ref[...] + (jnp.minimum(f, 0.0) - sp)
            la = la_ref[...]
            lf = jnp.maximum(la, u) + jnp.log(1.0 + jnp.exp2(_neg_abs(la - u) * LOG2E))
            lf_scr[sl, :] = lf * LOG2E
            k_scr[sl, :] = omlb_ref[...] * jnp.exp(jnp.minimum(-f, 0.0) - sp)
        return carry

    row = lax.broadcasted_iota(jnp.int32, (C, HEAD_DIM), 0)
    n_levels = msk_ref.shape[0]

    def intra_body(it, carry):
        chunks = [it * HGRN_UNROLL + u for u in range(HGRN_UNROLL)]
        sls = [pl.ds(c * C, C) for c in chunks]
        tri = tri_ref[...]
        gs = []
        for sl in sls:
            lf = lf_scr[sl, :]
            hi = lf.astype(BF16)
            r1 = lf - hi.astype(F32)
            mid = r1.astype(BF16)
            lo = (r1 - mid.astype(F32)).astype(BF16)
            gs.append(_dot(tri, hi) + _dot(tri, mid) + _dot(tri, lo))
        scores, kds = [], []
        for u, (c, sl, G) in enumerate(zip(chunks, sls, gs)):
            g_scr[u] = G
            q = q_scr[sl, :]
            k = k_scr[sl, :]
            f = jnp.exp2(lf_scr[sl, :])
            a = msk_ref[0] * _dot_nt(q.astype(BF16), k.astype(BF16))
            x = jnp.where((row & 1) == 1, q * f, k).astype(BF16)
            a = a + msk_ref[1] * _dot_nt(x, x)
            r4 = row & 3
            dec = jnp.where(r4 == 3, f * pltpu.roll(f, 1, 0),
                            jnp.where(r4 == 2, f, jnp.where(r4 == 0, pltpu.roll(f, C - 1, 0), 1.0)))
            x = (jnp.where(r4 >= 2, q, k) * dec).astype(BF16)
            a = a + msk_ref[2] * _dot_nt(x, x)
            for lvl in range(3, n_levels):
                m = 2 ** (lvl - 1)
                gref = jnp.concatenate(
                    [jnp.broadcast_to(g_scr[u, pl.ds(i * 2 * m + m - 1, 1), :], (2 * m, HEAD_DIM))
                     for i in range(C // (2 * m))], axis=0)
                if m >= 8:
                    qk = jnp.concatenate(
                        [part for i in range(C // (2 * m))
                         for part in (k[i * 2 * m:i * 2 * m + m], q[i * 2 * m + m:(i + 1) * 2 * m])], axis=0)
                else:
                    qk = jnp.where((row & m) != 0, q, k)
                x = (qk * jnp.exp2(_neg_abs(G - gref))).astype(BF16)
                a = a + msk_ref[lvl] * _dot_nt(x, x)
            scores.append(a)
            glast = g_scr[u, C - 1:C, :]
            qg_scr[sl, :] = (q * jnp.exp2(G)).astype(BF16)
            kds.append((k * jnp.exp2(glast - G)).astype(BF16))
            eg_scr[c] = jnp.exp2(glast)
        for c, sl, a, kd in zip(chunks, sls, scores, kds):
            v = v_scr[sl, :]
            oi_scr[sl, :] = _dot(a.astype(BF16), v.astype(BF16))
            ds_scr[c] = _dot(v.T.astype(BF16), kd)
        return carry

    n_units = seq // (tm * HGRN_PROJ_TILES)
    assert tm * HGRN_PROJ_TILES == C * HGRN_UNROLL
    proj_body(0, 0)
    for t in range(1, n_units):
        proj_body(t, 0)
        intra_body(t - 1, 0)
    intra_body(n_units - 1, 0)

    def scan_body(c, st):
        st_scr[c] = st.astype(BF16)
        return st * eg_scr[c] + ds_scr[c]

    lax.fori_loop(0, n_chunks, scan_body, jnp.zeros((HEAD_DIM, HEAD_DIM), F32))

    def out_body(it, carry):
        chunks = [it * HGRN_OUT_UNROLL + u for u in range(HGRN_OUT_UNROLL)]
        sls = [pl.ds(pl.multiple_of(c * C, C), C) for c in chunks]
        outs = [oi_scr[sl, :] + _dot_nt(qg_scr[sl, :], st_scr[c]) for c, sl in zip(chunks, sls)]
        for sl, o in zip(sls, outs):
            y = _rms(o) * og_ref[...] * _silu(gate_scr[sl, :])
            y_ref[0, sl, :] = y.astype(BF16)
        return carry

    lax.fori_loop(0, n_chunks // HGRN_OUT_UNROLL, out_body, 0)


def _hgrn_mixer(h, w_heads, lb, o_gain):
    B, S, D = h.shape
    n_heads = D // HEAD_DIM
    per_head = w_heads.shape[1] // n_heads
    C = HGRN_CHUNK
    tri = jnp.asarray(np.tril(np.ones((C, C), np.float32)), BF16)
    masks = jnp.asarray(_hgrn_level_masks(C))
    row = lambda a: a[None, :]
    lane_spec = pl.BlockSpec((1, HEAD_DIM), lambda b, hd: (0, hd))
    seq_scr = pltpu.VMEM((S, HEAD_DIM), F32)
    return pl.pallas_call(
        functools.partial(_hgrn_kernel, seq=S),
        grid=(B, n_heads),
        in_specs=[
            pl.BlockSpec((1, S, D), lambda b, hd: (b, 0, 0)),
            pl.BlockSpec((D, per_head), lambda b, hd: (0, hd)),
            lane_spec, lane_spec, lane_spec,
            pl.BlockSpec((1, HEAD_DIM), lambda b, hd: (0, 0)),
            pl.BlockSpec((C, C), lambda b, hd: (0, 0)),
            pl.BlockSpec(masks.shape, lambda b, hd: (0, 0, 0)),
        ],
        out_specs=pl.BlockSpec((1, S, HEAD_DIM), lambda b, hd: (b, 0, hd)),
        out_shape=jax.ShapeDtypeStruct((B, S, D), BF16),
        scratch_shapes=[seq_scr] * 5 + [
            pltpu.VMEM((HGRN_UNROLL, C, HEAD_DIM), F32),
            seq_scr,
            pltpu.VMEM((S, HEAD_DIM), BF16),
            pltpu.VMEM((S // C, HEAD_DIM, HEAD_DIM), F32),
            pltpu.VMEM((S // C, 1, HEAD_DIM), F32),
            pltpu.VMEM((S // C, HEAD_DIM, HEAD_DIM), BF16),
        ],
        compiler_params=_params(("parallel", "arbitrary")),
        name="hgrn_mixer",
    )(h, w_heads, row(jnp.log(lb)), row(jnp.log1p(-lb)), row(1.0 - lb), row(o_gain), tri, masks)


def _out_kernel(x_ref, ym_ref, ye_ref, w_ref, g_ref, xo_ref, *h_ref, mix_width):
    acc = x_ref[...] + _dot(ym_ref[...], w_ref[:mix_width, :]) + _dot(ye_ref[...], w_ref[mix_width:, :])
    xo_ref[...] = acc
    if h_ref:
        h_ref[0][...] = (_rms(acc) * g_ref[...]).astype(BF16)


def _out_proj(x, y_mix, y_mem, w_out, next_gain):
    N, D = x.shape
    tm = 512
    wm, we = y_mix.shape[1], y_mem.shape[1]
    want_h = next_gain is not None
    gain = next_gain if want_h else jnp.ones((D,), F32)
    tile = lambda w: pl.BlockSpec((tm, w), lambda i: (i, 0))
    out_specs = [tile(D)] + ([tile(D)] if want_h else [])
    out_shape = [jax.ShapeDtypeStruct((N, D), F32)] + ([jax.ShapeDtypeStruct((N, D), BF16)] if want_h else [])
    res = pl.pallas_call(
        functools.partial(_out_kernel, mix_width=wm),
        grid=(N // tm,),
        in_specs=[tile(D), tile(wm), tile(we),
                  pl.BlockSpec((wm + we, D), lambda i: (0, 0)),
                  pl.BlockSpec((1, D), lambda i: (0, 0))],
        out_specs=out_specs,
        out_shape=out_shape,
        compiler_params=_params(("parallel",)),
        name="out_proj",
    )(x, y_mix, y_mem, w_out.astype(BF16), gain[None, :])
    return (res[0], res[1]) if want_h else (res[0], None)


def _split_in_weights(w, n_mix_parts, n_heads, n_rotary_parts=0):
    D = w.shape[0]
    wb = w.astype(BF16)
    mix = wb[:, :n_mix_parts * D].reshape(D, n_mix_parts, n_heads, HEAD_DIM)
    if n_rotary_parts:
        mix = jnp.concatenate([_pair_lanes(mix[:, :n_rotary_parts]), mix[:, n_rotary_parts:]], axis=1)
    q_mem = wb[:, n_mix_parts * D:n_mix_parts * D + MEM_WIDTH]
    gate = wb[:, n_mix_parts * D + MEM_WIDTH:]
    gate_mix = gate[:, :D].reshape(D, 1, n_heads, HEAD_DIM)
    heads = jnp.concatenate([mix, gate_mix], axis=1).transpose(0, 2, 1, 3)
    return heads.reshape(D, n_heads * (n_mix_parts + 1) * HEAD_DIM), jnp.concatenate([q_mem, gate[:, D:]], axis=1)


def kernel(x, mem, positions, norm_gain, w_in_a, q_gain_a, k_gain_a, w_out_a, w_in_b, lb_logits,
           o_gain_b, w_out_b, mem_norm_gain, w_mem_kv, mem_q_gain, mem_k_gain):
    B, S, D = x.shape
    depth = norm_gain.shape[0]
    n_heads = D // HEAD_DIM

    sm = jax.nn.softmax(lb_logits.astype(F32), axis=0)
    lower_bounds = jnp.cumsum(sm, axis=0) - sm[0:1]

    h, cos, sin = _prologue(x, positions, norm_gain[0])
    km, vm = _memkv(mem, mem_norm_gain, w_mem_kv, mem_k_gain)

    xf = x.reshape(B * S, D)
    for l in range(depth):
        j = l // 2
        if l % 2 == 0:
            w_heads, w_mem = _split_in_weights(w_in_a[j], 3 * N_GROUPS, n_heads, 2 * N_GROUPS)
            y_mix = _attn_mixer(h, w_heads, cos, sin, _pair_lanes(q_gain_a[j]), _pair_lanes(k_gain_a[j]))
            w_out = w_out_a[j]
        else:
            w_heads, w_mem = _split_in_weights(w_in_b[j], 3, n_heads)
            y_mix = _hgrn_mixer(h, w_heads, lower_bounds[l], o_gain_b[j])
            w_out = w_out_b[j]
        y_mem = _memattn(h, w_mem, km, vm, l, mem_q_gain[l])
        next_gain = norm_gain[l + 1] if l + 1 < depth else None
        xf, hf = _out_proj(xf, y_mix.reshape(B * S, D), y_mem.reshape(B * S, MEM_WIDTH), w_out, next_gain)
        h = hf.reshape(B, S, D) if hf is not None else None
    return xf.reshape(B, S, D)
```

```python
import functools
import math

import numpy as np
import jax
import jax.numpy as jnp
from jax import lax
from jax.experimental import pallas as pl
from jax.experimental.pallas import tpu as pltpu

F32 = jnp.float32
BF16 = jnp.bfloat16

HEAD_DIM = 128
ROT_DIM = HEAD_DIM // 4
ROT_HALF = ROT_DIM // 2
ROPE_THETA = 500000.0
DIL_PAIRS = ((128, 1), (512, 4), (2048, 16))
N_GROUPS = len(DIL_PAIRS)
N_BACK = 128
MEM_HEADS = 4
MEM_WIDTH = MEM_HEADS * HEAD_DIM
EPS = 1e-6
ATTN_SCALE = 1.0 / math.sqrt(HEAD_DIM)
NEG_BIG = -1e30
HGRN_CHUNK = 128
HGRN_UNROLL = 4
HGRN_PROJ_TILES = 2
HGRN_OUT_UNROLL = 4
LOG2E = 1.4426950408889634
ATTN_UNROLL = 8
ATTN_PROJ_TILES = 2
VMEM_LIMIT = 52 * 1024 * 1024

_NT = (((1,), (1,)), ((), ()))


def _dot(a, b):
    return jnp.dot(a, b, preferred_element_type=F32)


def _dot_nt(a, b):
    return lax.dot_general(a, b, _NT, preferred_element_type=F32)


def _rms(x):
    return x * lax.rsqrt(jnp.mean(x * x, axis=-1, keepdims=True) + EPS)


def _silu(g):
    return g * (1.0 / (1.0 + jnp.exp(-g)))


def _params(sem):
    return pltpu.CompilerParams(dimension_semantics=sem, vmem_limit_bytes=VMEM_LIMIT)


def _prologue_kernel(x_ref, pos_ref, g_ref, freq_ref, sgn_ref, h_ref, cos_ref, sin_ref):
    h_ref[0] = (_rms(x_ref[0]) * g_ref[...]).astype(BF16)
    ang = pos_ref[0] * freq_ref[...]
    cos_ref[0] = jnp.cos(ang)
    sin_ref[0] = jnp.sin(ang) * sgn_ref[...]


def _prologue(x, positions, gain0):
    B, S, D = x.shape
    ts = 512
    inv_freq = ROPE_THETA ** (-jnp.arange(0, ROT_DIM, 2, dtype=F32) / ROT_DIM)
    freq = _pair_lanes(jnp.concatenate([inv_freq, inv_freq, jnp.zeros((HEAD_DIM - ROT_DIM,), F32)]))[None, :]
    sgn = jnp.asarray(np.where(np.arange(HEAD_DIM) < ROT_HALF, -1.0, 1.0), F32)[None, :]
    pos = positions.astype(F32)[..., None]
    return pl.pallas_call(
        _prologue_kernel,
        grid=(B, S // ts),
        in_specs=[
            pl.BlockSpec((1, ts, D), lambda b, t: (b, t, 0)),
            pl.BlockSpec((1, ts, 1), lambda b, t: (b, t, 0)),
            pl.BlockSpec((1, D), lambda b, t: (0, 0)),
            pl.BlockSpec((1, HEAD_DIM), lambda b, t: (0, 0)),
            pl.BlockSpec((1, HEAD_DIM), lambda b, t: (0, 0)),
        ],
        out_specs=[
            pl.BlockSpec((1, ts, D), lambda b, t: (b, t, 0)),
            pl.BlockSpec((1, ts, HEAD_DIM), lambda b, t: (b, t, 0)),
            pl.BlockSpec((1, ts, HEAD_DIM), lambda b, t: (b, t, 0)),
        ],
        out_shape=[
            jax.ShapeDtypeStruct((B, S, D), BF16),
            jax.ShapeDtypeStruct((B, S, HEAD_DIM), F32),
            jax.ShapeDtypeStruct((B, S, HEAD_DIM), F32),
        ],
        compiler_params=_params(("parallel", "parallel")),
        name="prologue",
    )(x, pos, gain0[None, :], freq, sgn)


def _memkv_kernel(mem_ref, mg_ref, w_ref, kg_ref, km_ref, vm_ref):
    mn = (_rms(mem_ref[0]) * mg_ref[0]).astype(BF16)
    kv = _dot(mn, w_ref[0])
    for hd in range(MEM_HEADS):
        sl = slice(hd * HEAD_DIM, (hd + 1) * HEAD_DIM)
        km_ref[0, 0, :, sl] = (_rms(kv[:, sl]) * kg_ref[0]).astype(BF16)
    vm_ref[0, 0] = kv[:, MEM_WIDTH:].astype(BF16)


def _memkv(mem, mem_norm_gain, w_mem_kv, mem_k_gain):
    B, M, D = mem.shape
    depth = w_mem_kv.shape[0]
    out = jax.ShapeDtypeStruct((depth, B, M, MEM_WIDTH), BF16)
    return pl.pallas_call(
        _memkv_kernel,
        grid=(depth, B),
        in_specs=[
            pl.BlockSpec((1, M, D), lambda l, b: (b, 0, 0)),
            pl.BlockSpec((1, 1, D), lambda l, b: (l, 0, 0)),
            pl.BlockSpec((1, D, 2 * MEM_WIDTH), lambda l, b: (l, 0, 0)),
            pl.BlockSpec((1, 1, HEAD_DIM), lambda l, b: (l, 0, 0)),
        ],
        out_specs=[
            pl.BlockSpec((1, 1, M, MEM_WIDTH), lambda l, b: (l, b, 0, 0)),
            pl.BlockSpec((1, 1, M, MEM_WIDTH), lambda l, b: (l, b, 0, 0)),
        ],
        out_shape=[out, out],
        compiler_params=_params(("parallel", "parallel")),
        name="memkv",
    )(mem, mem_norm_gain[:, None, :], w_mem_kv.astype(BF16), mem_k_gain[:, None, :])


def _memattn_kernel(h_ref, w_ref, km_ref, vm_ref, qg_ref, y_ref):
    ht = h_ref[0]
    q_all = _dot(ht, w_ref[:, :MEM_WIDTH])
    gate = _dot(ht, w_ref[:, MEM_WIDTH:])
    for hd in range(MEM_HEADS):
        sl = slice(hd * HEAD_DIM, (hd + 1) * HEAD_DIM)
        qn = (_rms(q_all[:, sl]) * (qg_ref[...] * ATTN_SCALE)).astype(BF16)
        s = _dot_nt(qn, km_ref[0, 0, :, sl])
        m = jnp.max(s, axis=-1, keepdims=True)
        p = jnp.exp(s - m)
        den = jnp.sum(p, axis=-1, keepdims=True)
        o = _dot(p.astype(BF16), vm_ref[0, 0, :, sl]) * (1.0 / den)
        y_ref[0, :, sl] = (o * _silu(gate[:, sl])).astype(BF16)


def _memattn(h, w_mem, km, vm, layer, q_gain):
    B, S, D = h.shape
    M = km.shape[2]
    ts = 512
    return pl.pallas_call(
        _memattn_kernel,
        grid=(B, S // ts),
        in_specs=[
            pl.BlockSpec((1, ts, D), lambda b, t: (b, t, 0)),
            pl.BlockSpec((D, 2 * MEM_WIDTH), lambda b, t: (0, 0)),
            pl.BlockSpec((1, 1, M, MEM_WIDTH), lambda b, t: (layer, b, 0, 0)),
            pl.BlockSpec((1, 1, M, MEM_WIDTH), lambda b, t: (layer, b, 0, 0)),
            pl.BlockSpec((1, HEAD_DIM), lambda b, t: (0, 0)),
        ],
        out_specs=pl.BlockSpec((1, ts, MEM_WIDTH), lambda b, t: (b, t, 0)),
        out_shape=jax.ShapeDtypeStruct((B, S, MEM_WIDTH), BF16),
        compiler_params=_params(("parallel", "parallel")),
        name="memattn",
    )(h, w_mem, km, vm, q_gain[None, :])


def _pair_lanes(x):
    half = HEAD_DIM // 2
    return jnp.concatenate([x[..., :ROT_HALF], x[..., ROT_DIM:half + ROT_HALF],
                            x[..., ROT_HALF:ROT_DIM], x[..., half + ROT_HALF:]], axis=-1)


def _rotary(x, cos, sin):
    return x * cos + pltpu.roll(x, HEAD_DIM // 2, 1) * sin


def _attn_bias():
    i = np.arange(N_BACK)[:, None]
    j = np.arange(2 * N_BACK)[None, :]
    band = (j >= i) & (j <= i + N_BACK)
    masks = np.stack([band, band & (j >= N_BACK), j <= i])
    return np.where(masks, 0.0, NEG_BIG).astype(np.float32)


def _attn_kernel(h_ref, w_ref, cos_ref, sin_ref, qg_ref, kg_ref, bias_ref, y_ref,
                 q0_scr, q1_scr, q2_scr, k0_scr, k1_scr, k2_scr, v0_scr, v1_scr, v2_scr,
                 o_scr, m_scr, den_scr, gate_scr, *, seq):
    q_scrs = (q0_scr, q1_scr, q2_scr)
    k_scrs = (k0_scr, k1_scr, k2_scr)
    v_scrs = (v0_scr, v1_scr, v2_scr)
    pads = tuple(N_BACK * d if seq // (N_BACK * d) > 1 else 0 for _, d in DIL_PAIRS)
    tm = 256

    for g in range(N_GROUPS):
        if pads[g]:
            k_scrs[g][pl.ds(0, pads[g]), :] = jnp.zeros((pads[g], HEAD_DIM), k_scrs[g].dtype)
            v_scrs[g][pl.ds(0, pads[g]), :] = jnp.zeros((pads[g], HEAD_DIM), v_scrs[g].dtype)

    def proj_body(t, carry):
        r0s = [pl.multiple_of((t * ATTN_PROJ_TILES + i) * tm, tm) for i in range(ATTN_PROJ_TILES)]
        hts = [h_ref[0, pl.ds(r0, tm), :] for r0 in r0s]
        for pair in range(5):
            w_pair = w_ref[:, pair * 256:(pair + 1) * 256]
            for r0, ht in zip(r0s, hts):
                cols = _dot(ht, w_pair)
                cos = cos_ref[0, pl.ds(r0, tm), :]
                sin = sin_ref[0, pl.ds(r0, tm), :]
                for half in range(2):
                    blk = 2 * pair + half
                    c = cols[:, half * HEAD_DIM:(half + 1) * HEAD_DIM]
                    if blk < 3:
                        g = blk
                        qn = _rms(c) * (qg_ref[g:g + 1, :] * (ATTN_SCALE * LOG2E))
                        q_scrs[g][pl.ds(r0, tm), :] = _rotary(qn, cos, sin).astype(q_scrs[g].dtype)
                    elif blk < 6:
                        g = blk - 3
                        kn = _rms(c) * kg_ref[g:g + 1, :]
                        k_scrs[g][pl.ds(pads[g] + r0, tm), :] = _rotary(kn, cos, sin).astype(k_scrs[g].dtype)
                    elif blk < 9:
                        g = blk - 6
                        v_scrs[g][pl.ds(pads[g] + r0, tm), :] = c.astype(v_scrs[g].dtype)
                    else:
                        gate_scr[pl.ds(r0, tm), :] = c
        return carry

    lax.fori_loop(0, seq // (tm * ATTN_PROJ_TILES), proj_body, 0)

    ones = jnp.ones((2 * N_BACK, HEAD_DIM), BF16)
    for g, (_, d) in enumerate(DIL_PAIRS):
        nb = seq // (N_BACK * d)
        nk = (2 if nb > 1 else 1) * N_BACK

        def rows(start, n, d=d):
            return pl.ds(start, n) if d == 1 else pl.ds(start, n, stride=d)

        for it in range(d * nb // ATTN_UNROLL):
            scores, outs = [], []
            for u in range(ATTN_UNROLL):
                r, c = divmod(it * ATTN_UNROLL + u, nb)
                start = c * (N_BACK * d) + r
                q = q_scrs[g][rows(start, N_BACK), :].astype(BF16)
                k = k_scrs[g][rows(start, nk), :].astype(BF16)
                if nb == 1:
                    bias = bias_ref[2, :, :N_BACK]
                else:
                    bias = bias_ref[1 if c == 0 else 0]
                scores.append((start, _dot_nt(q, k) + bias))
            for start, s in scores:
                m = jnp.max(s, axis=-1, keepdims=True)
                p = jnp.exp2(s - m).astype(BF16)
                v = v_scrs[g][rows(start, nk), :].astype(BF16)
                outs.append((start, _dot(p, jnp.concatenate([v, ones[:nk]], axis=1)), m))
            for start, od, m in outs:
                o_scr[g, rows(start, N_BACK), :] = od[:, :HEAD_DIM]
                m_scr[g, rows(start, N_BACK), :] = jnp.broadcast_to(m, (N_BACK, HEAD_DIM))
                den_scr[g, rows(start, N_BACK), :] = od[:, HEAD_DIM:]

    def merge_body(t, carry):
        r0 = pl.multiple_of(t * tm, tm)
        sl = pl.ds(r0, tm)
        m0, m1, m2 = m_scr[0, sl, :], m_scr[1, sl, :], m_scr[2, sl, :]
        mx = jnp.maximum(jnp.maximum(m0, m1), m2)
        w0, w1, w2 = jnp.exp2(m0 - mx), jnp.exp2(m1 - mx), jnp.exp2(m2 - mx)
        num = w0 * o_scr[0, sl, :] + w1 * o_scr[1, sl, :] + w2 * o_scr[2, sl, :]
        den = w0 * den_scr[0, sl, :] + w1 * den_scr[1, sl, :] + w2 * den_scr[2, sl, :]
        y_ref[0, sl, :] = (num * (1.0 / den) * _silu(gate_scr[sl, :])).astype(BF16)
        return carry

    lax.fori_loop(0, seq // tm, merge_body, 0)


def _attn_mixer(h, w_heads, cos, sin, q_gain, k_gain):
    B, S, D = h.shape
    n_heads = D // HEAD_DIM
    per_head = w_heads.shape[1] // n_heads
    assert S % (N_BACK * DIL_PAIRS[-1][1]) == 0
    pads = tuple(N_BACK * d if S // (N_BACK * d) > 1 else 0 for _, d in DIL_PAIRS)
    dts = [BF16 if d == 1 else F32 for _, d in DIL_PAIRS]
    q_scr = [pltpu.VMEM((S, HEAD_DIM), dt) for dt in dts]
    kv_scr = [pltpu.VMEM((p + S, HEAD_DIM), dt) for p, dt in zip(pads, dts)]
    grp_scr = pltpu.VMEM((N_GROUPS, S, HEAD_DIM), F32)
    bias = jnp.asarray(_attn_bias())
    return pl.pallas_call(
        functools.partial(_attn_kernel, seq=S),
        grid=(B, n_heads),
        in_specs=[
            pl.BlockSpec((1, S, D), lambda b, hd: (b, 0, 0)),
            pl.BlockSpec((D, per_head), lambda b, hd: (0, hd)),
            pl.BlockSpec((1, S, HEAD_DIM), lambda b, hd: (b, 0, 0)),
            pl.BlockSpec((1, S, HEAD_DIM), lambda b, hd: (b, 0, 0)),
            pl.BlockSpec((N_GROUPS, HEAD_DIM), lambda b, hd: (0, 0)),
            pl.BlockSpec((N_GROUPS, HEAD_DIM), lambda b, hd: (0, 0)),
            pl.BlockSpec(bias.shape, lambda b, hd: (0, 0, 0)),
        ],
        out_specs=pl.BlockSpec((1, S, HEAD_DIM), lambda b, hd: (b, 0, hd)),
        out_shape=jax.ShapeDtypeStruct((B, S, D), BF16),
        scratch_shapes=q_scr + kv_scr + kv_scr + [grp_scr, grp_scr, grp_scr,
                                                  pltpu.VMEM((S, HEAD_DIM), F32)],
        compiler_params=_params(("parallel", "arbitrary")),
        name="attn_mixer",
    )(h, w_heads, cos, sin, q_gain, k_gain, bias)


def _hgrn_level_masks(c):
    t = np.arange(c)[:, None]
    s = np.arange(c)[None, :]
    masks = [(t == s)]
    m = 1
    while m < c:
        masks.append((t // (2 * m) == s // (2 * m)) & (t % (2 * m) >= m) & (s % (2 * m) < m))
        m *= 2
    return np.stack(masks).astype(np.float32)


def _hgrn_kernel(h_ref, w_ref, la_ref, lc_ref, omlb_ref, og_ref, tri_ref, msk_ref, y_ref,
                 q_scr, lf_scr, k_scr, v_scr, gate_scr, g_scr, oi_scr, qg_scr, ds_scr,
                 eg_scr, st_scr, *, seq):
    tm = 256
    C = HGRN_CHUNK
    n_chunks = seq // C

    def proj_body(t, carry):
        sls = [pl.ds((t * HGRN_PROJ_TILES + i) * tm, tm) for i in range(HGRN_PROJ_TILES)]
        for sl in sls:
            ht = h_ref[0, sl, :]
            qf = _dot(ht, w_ref[:, :256])
            vg = _dot(ht, w_ref[:, 256:])
            f = qf[:, HEAD_DIM:]
            q_scr[sl, :] = qf[:, :HEAD_DIM]
            v_scr[sl, :] = vg[:, :HEAD_DIM]
            gate_scr[sl, :] = vg[:, HEAD_DIM:]
            sp = jnp.log(1.0 + jnp.exp2(jnp.abs(f) * -LOG2E))
            u = lc_ref[...] + (jnp.minimum(f, 0.0) - sp)
            la = la_ref[...]
            lf = jnp.maximum(la, u) + jnp.log(1.0 + jnp.exp2(jnp.abs(la - u) * -LOG2E))
            lf_scr[sl, :] = lf * LOG2E
            k_scr[sl, :] = omlb_ref[...] * jnp.exp(jnp.minimum(-f, 0.0) - sp)
        return carry

    row = lax.broadcasted_iota(jnp.int32, (C, HEAD_DIM), 0)
    n_levels = msk_ref.shape[0]

    def intra_body(it, carry):
        chunks = [it * HGRN_UNROLL + u for u in range(HGRN_UNROLL)]
        sls = [pl.ds(c * C, C) for c in chunks]
        tri = tri_ref[...]
        gs = []
        for sl in sls:
            lf = lf_scr[sl, :]
            hi = lf.astype(BF16)
            r1 = lf - hi.astype(F32)
            mid = r1.astype(BF16)
            lo = (r1 - mid.astype(F32)).astype(BF16)
            gs.append(_dot(tri, hi) + _dot(tri, mid) + _dot(tri, lo))
        scores, kds = [], []
        for u, (c, sl, G) in enumerate(zip(chunks, sls, gs)):
            g_scr[u] = G
            q = q_scr[sl, :]
            k = k_scr[sl, :]
            f = jnp.exp2(lf_scr[sl, :])
            a = msk_ref[0] * _dot_nt(q.astype(BF16), k.astype(BF16))
            x = jnp.where((row & 1) == 1, q * f, k).astype(BF16)
            a = a + msk_ref[1] * _dot_nt(x, x)
            r4 = row & 3
            dec = jnp.where(r4 == 3, f * pltpu.roll(f, 1, 0),
                            jnp.where(r4 == 2, f, jnp.where(r4 == 0, pltpu.roll(f, C - 1, 0), 1.0)))
            x = (jnp.where(r4 >= 2, q, k) * dec).astype(BF16)
            a = a + msk_ref[2] * _dot_nt(x, x)
            for lvl in range(3, n_levels):
                m = 2 ** (lvl - 1)
                ref_rows = [g_scr[u, pl.ds(i * 2 * m + m - 1, 1), :] for i in range(C // (2 * m))]
                if m >= 8:
                    parts = []
                    for i, ref_row in enumerate(ref_rows):
                        gref = jnp.broadcast_to(ref_row, (m, HEAD_DIM))
                        lo_rows = slice(i * 2 * m, i * 2 * m + m)
                        up_rows = slice(i * 2 * m + m, (i + 1) * 2 * m)
                        parts.append(k[lo_rows] * jnp.exp2(gref - G[lo_rows]))
                        parts.append(q[up_rows] * jnp.exp2(G[up_rows] - gref))
                    x = jnp.concatenate(parts, axis=0).astype(BF16)
                else:
                    gref = jnp.concatenate(
                        [jnp.broadcast_to(ref_row, (2 * m, HEAD_DIM)) for ref_row in ref_rows], axis=0)
                    upper = (row & m) != 0
                    d = G - gref
                    x = (jnp.where(upper, q, k) * jnp.exp2(jnp.where(upper, d, -d))).astype(BF16)
                a = a + msk_ref[lvl] * _dot_nt(x, x)
            scores.append(a)
            glast = g_scr[u, C - 1:C, :]
            qg_scr[sl, :] = (q * jnp.exp2(G)).astype(BF16)
            kds.append((k * jnp.exp2(glast - G)).astype(BF16))
            eg_scr[c] = jnp.exp2(glast)
        for c, sl, a, kd in zip(chunks, sls, scores, kds):
            v = v_scr[sl, :]
            oi_scr[sl, :] = _dot(a.astype(BF16), v.astype(BF16))
            ds_scr[c] = _dot(v.T.astype(BF16), kd)
        return carry

    n_units = seq // (tm * HGRN_PROJ_TILES)
    assert tm * HGRN_PROJ_TILES == C * HGRN_UNROLL
    proj_body(0, 0)
    for t in range(1, n_units):
        proj_body(t, 0)
        intra_body(t - 1, 0)
    intra_body(n_units - 1, 0)

    def scan_body(c, st):
        st_scr[c] = st.astype(BF16)
        return st * eg_scr[c] + ds_scr[c]

    lax.fori_loop(0, n_chunks, scan_body, jnp.zeros((HEAD_DIM, HEAD_DIM), F32))

    def out_body(it, carry):
        chunks = [it * HGRN_OUT_UNROLL + u for u in range(HGRN_OUT_UNROLL)]
        sls = [pl.ds(pl.multiple_of(c * C, C), C) for c in chunks]
        outs = [oi_scr[sl, :] + _dot_nt(qg_scr[sl, :], st_scr[c]) for c, sl in zip(chunks, sls)]
        for sl, o in zip(sls, outs):
            y = _rms(o) * og_ref[...] * _silu(gate_scr[sl, :])
            y_ref[0, sl, :] = y.astype(BF16)
        return carry

    lax.fori_loop(0, n_chunks // HGRN_OUT_UNROLL, out_body, 0)


def _hgrn_mixer(h, w_heads, lb, o_gain):
    B, S, D = h.shape
    n_heads = D // HEAD_DIM
    per_head = w_heads.shape[1] // n_heads
    C = HGRN_CHUNK
    tri = jnp.asarray(np.tril(np.ones((C, C), np.float32)), BF16)
    masks = jnp.asarray(_hgrn_level_masks(C))
    row = lambda a: a[None, :]
    lane_spec = pl.BlockSpec((1, HEAD_DIM), lambda b, hd: (0, hd))
    seq_scr = pltpu.VMEM((S, HEAD_DIM), F32)
    return pl.pallas_call(
        functools.partial(_hgrn_kernel, seq=S),
        grid=(B, n_heads),
        in_specs=[
            pl.BlockSpec((1, S, D), lambda b, hd: (b, 0, 0)),
            pl.BlockSpec((D, per_head), lambda b, hd: (0, hd)),
            lane_spec, lane_spec, lane_spec,
            pl.BlockSpec((1, HEAD_DIM), lambda b, hd: (0, 0)),
            pl.BlockSpec((C, C), lambda b, hd: (0, 0)),
            pl.BlockSpec(masks.shape, lambda b, hd: (0, 0, 0)),
        ],
        out_specs=pl.BlockSpec((1, S, HEAD_DIM), lambda b, hd: (b, 0, hd)),
        out_shape=jax.ShapeDtypeStruct((B, S, D), BF16),
        scratch_shapes=[seq_scr] * 5 + [
            pltpu.VMEM((HGRN_UNROLL, C, HEAD_DIM), F32),
            seq_scr,
            pltpu.VMEM((S, HEAD_DIM), BF16),
            pltpu.VMEM((S // C, HEAD_DIM, HEAD_DIM), F32),
            pltpu.VMEM((S // C, 1, HEAD_DIM), F32),
            pltpu.VMEM((S // C, HEAD_DIM, HEAD_DIM), BF16),
        ],
        compiler_params=_params(("parallel", "arbitrary")),
        name="hgrn_mixer",
    )(h, w_heads, row(jnp.log(lb)), row(jnp.log1p(-lb)), row(1.0 - lb), row(o_gain), tri, masks)


def _out_kernel(x_ref, ym_ref, ye_ref, w_ref, g_ref, xo_ref, *h_ref, mix_width):
    acc = x_ref[...] + _dot(ym_ref[...], w_ref[:mix_width, :]) + _dot(ye_ref[...], w_ref[mix_width:, :])
    xo_ref[...] = acc
    if h_ref:
        h_ref[0][...] = (_rms(acc) * g_ref[...]).astype(BF16)


def _out_proj(x, y_mix, y_mem, w_out, next_gain):
    N, D = x.shape
    tm = 512
    wm, we = y_mix.shape[1], y_mem.shape[1]
    want_h = next_gain is not None
    gain = next_gain if want_h else jnp.ones((D,), F32)
    tile = lambda w: pl.BlockSpec((tm, w), lambda i: (i, 0))
    out_specs = [tile(D)] + ([tile(D)] if want_h else [])
    out_shape = [jax.ShapeDtypeStruct((N, D), F32)] + ([jax.ShapeDtypeStruct((N, D), BF16)] if want_h else [])
    res = pl.pallas_call(
        functools.partial(_out_kernel, mix_width=wm),
        grid=(N // tm,),
        in_specs=[tile(D), tile(wm), tile(we),
                  pl.BlockSpec((wm + we, D), lambda i: (0, 0)),
                  pl.BlockSpec((1, D), lambda i: (0, 0))],
        out_specs=out_specs,
        out_shape=out_shape,
        compiler_params=_params(("parallel",)),
        name="out_proj",
    )(x, y_mix, y_mem, w_out.astype(BF16), gain[None, :])
    return (res[0], res[1]) if want_h else (res[0], None)


def _regroup_kernel(*refs, n_rotary_parts):
    *in_refs, out_ref = refs
    for p, ref in enumerate(in_refs):
        blk = ref[...]
        if p < n_rotary_parts:
            blk = _pair_lanes(blk)
        out_ref[:, p * ref.shape[1]:(p + 1) * ref.shape[1]] = blk.astype(BF16)


def _split_in_weights(w_all, j, n_mix_parts, n_heads, n_rotary_parts=0):
    D = w_all.shape[1]
    gate_col = n_mix_parts * D + MEM_WIDTH
    strip = lambda col_block: pl.BlockSpec((None, D, HEAD_DIM), lambda hd: (j, 0, col_block + hd))
    specs = [strip(p * n_heads) for p in range(n_mix_parts)] + [strip(gate_col // HEAD_DIM)]
    per_head = len(specs) * HEAD_DIM
    heads = pl.pallas_call(
        functools.partial(_regroup_kernel, n_rotary_parts=n_rotary_parts),
        grid=(n_heads,),
        in_specs=specs,
        out_specs=pl.BlockSpec((D, per_head), lambda hd: (0, hd)),
        out_shape=jax.ShapeDtypeStruct((D, n_heads * per_head), BF16),
        compiler_params=_params(("parallel",)),
        name="regroup_heads",
    )(*([w_all] * len(specs)))
    wide = lambda col: pl.BlockSpec((None, D, MEM_WIDTH), lambda i: (j, 0, col // MEM_WIDTH))
    mem_cols = pl.pallas_call(
        functools.partial(_regroup_kernel, n_rotary_parts=0),
        grid=(1,),
        in_specs=[wide(n_mix_parts * D), wide(gate_col + D)],
        out_specs=pl.BlockSpec((D, 2 * MEM_WIDTH), lambda i: (0, 0)),
        out_shape=jax.ShapeDtypeStruct((D, 2 * MEM_WIDTH), BF16),
        compiler_params=_params(("arbitrary",)),
        name="regroup_mem",
    )(w_all, w_all)
    return heads, mem_cols


def kernel(x, mem, positions, norm_gain, w_in_a, q_gain_a, k_gain_a, w_out_a, w_in_b, lb_logits,
           o_gain_b, w_out_b, mem_norm_gain, w_mem_kv, mem_q_gain, mem_k_gain):
    B, S, D = x.shape
    depth = norm_gain.shape[0]
    n_heads = D // HEAD_DIM

    sm = jax.nn.softmax(lb_logits.astype(F32), axis=0)
    lower_bounds = jnp.cumsum(sm, axis=0) - sm[0:1]

    h, cos, sin = _prologue(x, positions, norm_gain[0])
    km, vm = _memkv(mem, mem_norm_gain, w_mem_kv, mem_k_gain)

    xf = x.reshape(B * S, D)
    for l in range(depth):
        j = l // 2
        if l % 2 == 0:
            w_heads, w_mem = _split_in_weights(w_in_a, j, 3 * N_GROUPS, n_heads, 2 * N_GROUPS)
            y_mix = _attn_mixer(h, w_heads, cos, sin, _pair_lanes(q_gain_a[j]), _pair_lanes(k_gain_a[j]))
            w_out = w_out_a[j]
        else:
            w_heads, w_mem = _split_in_weights(w_in_b, j, 3, n_heads)
            y_mix = _hgrn_mixer(h, w_heads, lower_bounds[l], o_gain_b[j])
            w_out = w_out_b[j]
        y_mem = _memattn(h, w_mem, km, vm, l, mem_q_gain[l])
        next_gain = norm_gain[l + 1] if l + 1 < depth else None
        xf, hf = _out_proj(xf, y_mix.reshape(B * S, D), y_mem.reshape(B * S, MEM_WIDTH), w_out, next_gain)
        h = hf.reshape(B, S, D) if hf is not None else None
    return xf.reshape(B, S, D)
```

```python
import functools
import math

import numpy as np
import jax
import jax.numpy as jnp
from jax import lax
from jax.experimental import pallas as pl
from jax.experimental.pallas import tpu as pltpu

F32 = jnp.float32
BF16 = jnp.bfloat16

HEAD_DIM = 128
ROT_DIM = HEAD_DIM // 4
ROT_HALF = ROT_DIM // 2
ROPE_THETA = 500000.0
DIL_PAIRS = ((128, 1), (512, 4), (2048, 16))
N_GROUPS = len(DIL_PAIRS)
N_BACK = 128
MEM_HEADS = 4
MEM_WIDTH = MEM_HEADS * HEAD_DIM
EPS = 1e-6
ATTN_SCALE = 1.0 / math.sqrt(HEAD_DIM)
NEG_BIG = -1e30
HGRN_CHUNK = 128
HGRN_UNROLL = 4
HGRN_PROJ_TILES = 2
HGRN_OUT_UNROLL = 4
LOG2E = 1.4426950408889634
ATTN_UNROLL = 8
ATTN_PROJ_TILES = 2
VMEM_LIMIT = 52 * 1024 * 1024

_NT = (((1,), (1,)), ((), ()))


def _dot(a, b):
    return jnp.dot(a, b, preferred_element_type=F32)


def _dot_nt(a, b):
    return lax.dot_general(a, b, _NT, preferred_element_type=F32)


def _rms(x):
    return x * lax.rsqrt(jnp.mean(x * x, axis=-1, keepdims=True) + EPS)


def _silu(g):
    return g * (1.0 / (1.0 + jnp.exp(-g)))


def _params(sem):
    return pltpu.CompilerParams(dimension_semantics=sem, vmem_limit_bytes=VMEM_LIMIT)


def _prologue_kernel(x_ref, pos_ref, g_ref, freq_ref, sgn_ref, h_ref, cos_ref, sin_ref):
    h_ref[0] = (_rms(x_ref[0]) * g_ref[...]).astype(BF16)
    ang = pos_ref[0] * freq_ref[...]
    cos_ref[0] = jnp.cos(ang)
    sin_ref[0] = jnp.sin(ang) * sgn_ref[...]


def _prologue(x, positions, gain0):
    B, S, D = x.shape
    ts = 512
    inv_freq = ROPE_THETA ** (-jnp.arange(0, ROT_DIM, 2, dtype=F32) / ROT_DIM)
    freq = _pair_lanes(jnp.concatenate([inv_freq, inv_freq, jnp.zeros((HEAD_DIM - ROT_DIM,), F32)]))[None, :]
    sgn = jnp.asarray(np.where(np.arange(HEAD_DIM) < ROT_HALF, -1.0, 1.0), F32)[None, :]
    pos = positions.astype(F32)[..., None]
    return pl.pallas_call(
        _prologue_kernel,
        grid=(B, S // ts),
        in_specs=[
            pl.BlockSpec((1, ts, D), lambda b, t: (b, t, 0)),
            pl.BlockSpec((1, ts, 1), lambda b, t: (b, t, 0)),
            pl.BlockSpec((1, D), lambda b, t: (0, 0)),
            pl.BlockSpec((1, HEAD_DIM), lambda b, t: (0, 0)),
            pl.BlockSpec((1, HEAD_DIM), lambda b, t: (0, 0)),
        ],
        out_specs=[
            pl.BlockSpec((1, ts, D), lambda b, t: (b, t, 0)),
            pl.BlockSpec((1, ts, HEAD_DIM), lambda b, t: (b, t, 0)),
            pl.BlockSpec((1, ts, HEAD_DIM), lambda b, t: (b, t, 0)),
        ],
        out_shape=[
            jax.ShapeDtypeStruct((B, S, D), BF16),
            jax.ShapeDtypeStruct((B, S, HEAD_DIM), F32),
            jax.ShapeDtypeStruct((B, S, HEAD_DIM), F32),
        ],
        compiler_params=_params(("parallel", "parallel")),
        name="prologue",
    )(x, pos, gain0[None, :], freq, sgn)


def _memkv_kernel(mem_ref, mg_ref, w_ref, kg_ref, km_ref, vm_ref):
    mn = (_rms(mem_ref[0]) * mg_ref[0]).astype(BF16)
    kv = _dot(mn, w_ref[0])
    for hd in range(MEM_HEADS):
        sl = slice(hd * HEAD_DIM, (hd + 1) * HEAD_DIM)
        km_ref[0, 0, :, sl] = (_rms(kv[:, sl]) * kg_ref[0]).astype(BF16)
    vm_ref[0, 0] = kv[:, MEM_WIDTH:].astype(BF16)


def _memkv(mem, mem_norm_gain, w_mem_kv, mem_k_gain):
    B, M, D = mem.shape
    depth = w_mem_kv.shape[0]
    out = jax.ShapeDtypeStruct((depth, B, M, MEM_WIDTH), BF16)
    return pl.pallas_call(
        _memkv_kernel,
        grid=(depth, B),
        in_specs=[
            pl.BlockSpec((1, M, D), lambda l, b: (b, 0, 0)),
            pl.BlockSpec((1, 1, D), lambda l, b: (l, 0, 0)),
            pl.BlockSpec((1, D, 2 * MEM_WIDTH), lambda l, b: (l, 0, 0)),
            pl.BlockSpec((1, 1, HEAD_DIM), lambda l, b: (l, 0, 0)),
        ],
        out_specs=[
            pl.BlockSpec((1, 1, M, MEM_WIDTH), lambda l, b: (l, b, 0, 0)),
            pl.BlockSpec((1, 1, M, MEM_WIDTH), lambda l, b: (l, b, 0, 0)),
        ],
        out_shape=[out, out],
        compiler_params=_params(("parallel", "parallel")),
        name="memkv",
    )(mem, mem_norm_gain[:, None, :], w_mem_kv.astype(BF16), mem_k_gain[:, None, :])


def _mem_out_kernel(x_ref, h_ref, ym_ref, wm_ref, km_ref, vm_ref, qg_ref, wo_ref, g_ref, xo_ref, *hn_ref):
    ht = h_ref[0]
    q_all = _dot(ht, wm_ref[:, :MEM_WIDTH])
    gate = _dot(ht, wm_ref[:, MEM_WIDTH:])
    y_heads = []
    for hd in range(MEM_HEADS):
        sl = slice(hd * HEAD_DIM, (hd + 1) * HEAD_DIM)
        qn = (_rms(q_all[:, sl]) * (qg_ref[...] * ATTN_SCALE)).astype(BF16)
        s = _dot_nt(qn, km_ref[0, 0, :, sl])
        m = jnp.max(s, axis=-1, keepdims=True)
        p = jnp.exp(s - m)
        den = jnp.sum(p, axis=-1, keepdims=True)
        o = _dot(p.astype(BF16), vm_ref[0, 0, :, sl]) * (1.0 / den)
        y_heads.append((o * _silu(gate[:, sl])).astype(BF16))
    y_mem = jnp.concatenate(y_heads, axis=1)
    mix_width = ym_ref.shape[2]
    acc = x_ref[0] + _dot(ym_ref[0], wo_ref[:mix_width, :]) + _dot(y_mem, wo_ref[mix_width:, :])
    xo_ref[0] = acc
    if hn_ref:
        hn_ref[0][0] = (_rms(acc) * g_ref[...]).astype(BF16)


def _mem_out(x, h, y_mix, w_mem, km, vm, layer, q_gain, w_out, next_gain):
    B, S, D = x.shape
    M = km.shape[2]
    ts = 512
    want_h = next_gain is not None
    gain = next_gain if want_h else jnp.ones((D,), F32)
    tile = pl.BlockSpec((1, ts, D), lambda b, t: (b, t, 0))
    mem_kv = pl.BlockSpec((1, 1, M, MEM_WIDTH), lambda b, t: (layer, b, 0, 0))
    fixed = lambda *shape: pl.BlockSpec(shape, lambda b, t: (0,) * len(shape))
    res = pl.pallas_call(
        _mem_out_kernel,
        grid=(B, S // ts),
        in_specs=[tile, tile, tile, fixed(D, 2 * MEM_WIDTH), mem_kv, mem_kv, fixed(1, HEAD_DIM),
                  fixed(D + MEM_WIDTH, D), fixed(1, D)],
        out_specs=[tile] + ([tile] if want_h else []),
        out_shape=[jax.ShapeDtypeStruct((B, S, D), F32)] + ([jax.ShapeDtypeStruct((B, S, D), BF16)] if want_h else []),
        compiler_params=_params(("parallel", "parallel")),
        name="mem_out",
    )(x, h, y_mix, w_mem, km, vm, q_gain[None, :], w_out.astype(BF16), gain[None, :])
    return (res[0], res[1]) if want_h else (res[0], None)


def _pair_lanes(x):
    half = HEAD_DIM // 2
    return jnp.concatenate([x[..., :ROT_HALF], x[..., ROT_DIM:half + ROT_HALF],
                            x[..., ROT_HALF:ROT_DIM], x[..., half + ROT_HALF:]], axis=-1)


def _rotary(x, cos, sin):
    return x * cos + pltpu.roll(x, HEAD_DIM // 2, 1) * sin


def _attn_bias():
    i = np.arange(N_BACK)[:, None]
    j = np.arange(2 * N_BACK)[None, :]
    band = (j >= i) & (j <= i + N_BACK)
    masks = np.stack([band, band & (j >= N_BACK), j <= i])
    return np.where(masks, 0.0, NEG_BIG).astype(np.float32)


def _attn_kernel(h_ref, w_ref, cos_ref, sin_ref, qg_ref, kg_ref, bias_ref, y_ref,
                 q0_scr, q1_scr, q2_scr, k0_scr, k1_scr, k2_scr, v0_scr, v1_scr, v2_scr,
                 o_scr, m_scr, den_scr, gate_scr, *, seq):
    q_scrs = (q0_scr, q1_scr, q2_scr)
    k_scrs = (k0_scr, k1_scr, k2_scr)
    v_scrs = (v0_scr, v1_scr, v2_scr)
    pads = tuple(N_BACK * d if seq // (N_BACK * d) > 1 else 0 for _, d in DIL_PAIRS)
    tm = 256

    for g in range(N_GROUPS):
        if pads[g]:
            k_scrs[g][pl.ds(0, pads[g]), :] = jnp.zeros((pads[g], HEAD_DIM), k_scrs[g].dtype)
            v_scrs[g][pl.ds(0, pads[g]), :] = jnp.zeros((pads[g], HEAD_DIM), v_scrs[g].dtype)

    def proj_body(t, carry):
        r0s = [pl.multiple_of((t * ATTN_PROJ_TILES + i) * tm, tm) for i in range(ATTN_PROJ_TILES)]
        hts = [h_ref[0, pl.ds(r0, tm), :] for r0 in r0s]
        for pair in range(5):
            w_pair = w_ref[:, pair * 256:(pair + 1) * 256]
            for r0, ht in zip(r0s, hts):
                cols = _dot(ht, w_pair)
                cos = cos_ref[0, pl.ds(r0, tm), :]
                sin = sin_ref[0, pl.ds(r0, tm), :]
                for half in range(2):
                    blk = 2 * pair + half
                    c = cols[:, half * HEAD_DIM:(half + 1) * HEAD_DIM]
                    if blk < 3:
                        g = blk
                        qn = _rms(c) * (qg_ref[g:g + 1, :] * (ATTN_SCALE * LOG2E))
                        q_scrs[g][pl.ds(r0, tm), :] = _rotary(qn, cos, sin).astype(q_scrs[g].dtype)
                    elif blk < 6:
                        g = blk - 3
                        kn = _rms(c) * kg_ref[g:g + 1, :]
                        k_scrs[g][pl.ds(pads[g] + r0, tm), :] = _rotary(kn, cos, sin).astype(k_scrs[g].dtype)
                    elif blk < 9:
                        g = blk - 6
                        v_scrs[g][pl.ds(pads[g] + r0, tm), :] = c.astype(v_scrs[g].dtype)
                    else:
                        gate_scr[pl.ds(r0, tm), :] = c
        return carry

    lax.fori_loop(0, seq // (tm * ATTN_PROJ_TILES), proj_body, 0)

    ones = jnp.ones((2 * N_BACK, HEAD_DIM), BF16)
    for g, (_, d) in enumerate(DIL_PAIRS):
        nb = seq // (N_BACK * d)
        nk = (2 if nb > 1 else 1) * N_BACK

        def rows(start, n, d=d):
            return pl.ds(start, n) if d == 1 else pl.ds(start, n, stride=d)

        for it in range(d * nb // ATTN_UNROLL):
            scores, outs = [], []
            for u in range(ATTN_UNROLL):
                r, c = divmod(it * ATTN_UNROLL + u, nb)
                start = c * (N_BACK * d) + r
                q = q_scrs[g][rows(start, N_BACK), :].astype(BF16)
                k = k_scrs[g][rows(start, nk), :].astype(BF16)
                if nb == 1:
                    bias = bias_ref[2, :, :N_BACK]
                else:
                    bias = bias_ref[1 if c == 0 else 0]
                scores.append((start, _dot_nt(q, k) + bias))
            for start, s in scores:
                m = jnp.max(s, axis=-1, keepdims=True)
                p = jnp.exp2(s - m).astype(BF16)
                v = v_scrs[g][rows(start, nk), :].astype(BF16)
                outs.append((start, _dot(p, jnp.concatenate([v, ones[:nk]], axis=1)), m))
            for start, od, m in outs:
                o_scr[g, rows(start, N_BACK), :] = od[:, :HEAD_DIM]
                m_scr[g, rows(start, N_BACK), :] = jnp.broadcast_to(m, (N_BACK, HEAD_DIM))
                den_scr[g, rows(start, N_BACK), :] = od[:, HEAD_DIM:]

    def merge_body(t, carry):
        r0 = pl.multiple_of(t * tm, tm)
        sl = pl.ds(r0, tm)
        m0, m1, m2 = m_scr[0, sl, :], m_scr[1, sl, :], m_scr[2, sl, :]
        mx = jnp.maximum(jnp.maximum(m0, m1), m2)
        w0, w1, w2 = jnp.exp2(m0 - mx), jnp.exp2(m1 - mx), jnp.exp2(m2 - mx)
        num = w0 * o_scr[0, sl, :] + w1 * o_scr[1, sl, :] + w2 * o_scr[2, sl, :]
        den = w0 * den_scr[0, sl, :] + w1 * den_scr[1, sl, :] + w2 * den_scr[2, sl, :]
        y_ref[0, sl, :] = (num * (1.0 / den) * _silu(gate_scr[sl, :])).astype(BF16)
        return carry

    lax.fori_loop(0, seq // tm, merge_body, 0)


def _attn_mixer(h, w_heads, cos, sin, q_gain, k_gain):
    B, S, D = h.shape
    n_heads = D // HEAD_DIM
    per_head = w_heads.shape[1] // n_heads
    assert S % (N_BACK * DIL_PAIRS[-1][1]) == 0
    pads = tuple(N_BACK * d if S // (N_BACK * d) > 1 else 0 for _, d in DIL_PAIRS)
    dts = [BF16 if d == 1 else F32 for _, d in DIL_PAIRS]
    q_scr = [pltpu.VMEM((S, HEAD_DIM), dt) for dt in dts]
    kv_scr = [pltpu.VMEM((p + S, HEAD_DIM), dt) for p, dt in zip(pads, dts)]
    grp_scr = pltpu.VMEM((N_GROUPS, S, HEAD_DIM), F32)
    bias = jnp.asarray(_attn_bias())
    return pl.pallas_call(
        functools.partial(_attn_kernel, seq=S),
        grid=(B, n_heads),
        in_specs=[
            pl.BlockSpec((1, S, D), lambda b, hd: (b, 0, 0)),
            pl.BlockSpec((D, per_head), lambda b, hd: (0, hd)),
            pl.BlockSpec((1, S, HEAD_DIM), lambda b, hd: (b, 0, 0)),
            pl.BlockSpec((1, S, HEAD_DIM), lambda b, hd: (b, 0, 0)),
            pl.BlockSpec((N_GROUPS, HEAD_DIM), lambda b, hd: (0, 0)),
            pl.BlockSpec((N_GROUPS, HEAD_DIM), lambda b, hd: (0, 0)),
            pl.BlockSpec(bias.shape, lambda b, hd: (0, 0, 0)),
        ],
        out_specs=pl.BlockSpec((1, S, HEAD_DIM), lambda b, hd: (b, 0, hd)),
        out_shape=jax.ShapeDtypeStruct((B, S, D), BF16),
        scratch_shapes=q_scr + kv_scr + kv_scr + [grp_scr, grp_scr, grp_scr,
                                                  pltpu.VMEM((S, HEAD_DIM), F32)],
        compiler_params=_params(("parallel", "arbitrary")),
        name="attn_mixer",
    )(h, w_heads, cos, sin, q_gain, k_gain, bias)


def _hgrn_level_masks(c):
    t = np.arange(c)[:, None]
    s = np.arange(c)[None, :]
    masks = [(t == s)]
    m = 1
    while m < c:
        masks.append((t // (2 * m) == s // (2 * m)) & (t % (2 * m) >= m) & (s % (2 * m) < m))
        m *= 2
    return np.stack(masks).astype(np.float32)


def _hgrn_kernel(h_ref, w_ref, la_ref, lc_ref, omlb_ref, og_ref, tri_ref, msk_ref, y_ref,
                 q_scr, lf_scr, k_scr, v_scr, gate_scr, g_scr, oi_scr, qg_scr, ds_scr,
                 eg_scr, st_scr, *, seq):
    tm = 256
    C = HGRN_CHUNK
    n_chunks = seq // C

    def proj_body(t, carry):
        sls = [pl.ds((t * HGRN_PROJ_TILES + i) * tm, tm) for i in range(HGRN_PROJ_TILES)]
        for sl in sls:
            ht = h_ref[0, sl, :]
            qf = _dot(ht, w_ref[:, :256])
            vg = _dot(ht, w_ref[:, 256:])
            f = qf[:, HEAD_DIM:]
            q_scr[sl, :] = qf[:, :HEAD_DIM]
            v_scr[sl, :] = vg[:, :HEAD_DIM]
            gate_scr[sl, :] = vg[:, HEAD_DIM:]
            sp = jnp.log(1.0 + jnp.exp2(jnp.abs(f) * -LOG2E))
            u = lc_ref[...] + (jnp.minimum(f, 0.0) - sp)
            la = la_ref[...]
            lf = jnp.maximum(la, u) + jnp.log(1.0 + jnp.exp2(jnp.abs(la - u) * -LOG2E))
            lf_scr[sl, :] = lf * LOG2E
            k_scr[sl, :] = omlb_ref[...] * jnp.exp(jnp.minimum(-f, 0.0) - sp)
        return carry

    row = lax.broadcasted_iota(jnp.int32, (C, HEAD_DIM), 0)
    n_levels = msk_ref.shape[0]

    def intra_body(it, carry):
        chunks = [it * HGRN_UNROLL + u for u in range(HGRN_UNROLL)]
        sls = [pl.ds(c * C, C) for c in chunks]
        tri = tri_ref[...]
        gs = []
        for sl in sls:
            lf = lf_scr[sl, :]
            hi = lf.astype(BF16)
            r1 = lf - hi.astype(F32)
            mid = r1.astype(BF16)
            lo = (r1 - mid.astype(F32)).astype(BF16)
            gs.append(_dot(tri, hi) + _dot(tri, mid) + _dot(tri, lo))
        scores, kds = [], []
        for u, (c, sl, G) in enumerate(zip(chunks, sls, gs)):
            g_scr[u] = G
            q = q_scr[sl, :]
            k = k_scr[sl, :]
            f = jnp.exp2(lf_scr[sl, :])
            a = msk_ref[0] * _dot_nt(q.astype(BF16), k.astype(BF16))
            x = jnp.where((row & 1) == 1, q * f, k).astype(BF16)
            a = a + msk_ref[1] * _dot_nt(x, x)
            r4 = row & 3
            dec = jnp.where(r4 == 3, f * pltpu.roll(f, 1, 0),
                            jnp.where(r4 == 2, f, jnp.where(r4 == 0, pltpu.roll(f, C - 1, 0), 1.0)))
            x = (jnp.where(r4 >= 2, q, k) * dec).astype(BF16)
            a = a + msk_ref[2] * _dot_nt(x, x)
            for lvl in range(3, n_levels):
                m = 2 ** (lvl - 1)
                ref_rows = [g_scr[u, pl.ds(i * 2 * m + m - 1, 1), :] for i in range(C // (2 * m))]
                if m >= 8:
                    parts = []
                    for i, ref_row in enumerate(ref_rows):
                        gref = jnp.broadcast_to(ref_row, (m, HEAD_DIM))
                        lo_rows = slice(i * 2 * m, i * 2 * m + m)
                        up_rows = slice(i * 2 * m + m, (i + 1) * 2 * m)
                        parts.append(k[lo_rows] * jnp.exp2(gref - G[lo_rows]))
                        parts.append(q[up_rows] * jnp.exp2(G[up_rows] - gref))
                    x = jnp.concatenate(parts, axis=0).astype(BF16)
                else:
                    gref = jnp.concatenate(
                        [jnp.broadcast_to(ref_row, (2 * m, HEAD_DIM)) for ref_row in ref_rows], axis=0)
                    upper = (row & m) != 0
                    d = G - gref
                    x = (jnp.where(upper, q, k) * jnp.exp2(jnp.where(upper, d, -d))).astype(BF16)
                a = a + msk_ref[lvl] * _dot_nt(x, x)
            scores.append(a)
            glast = g_scr[u, C - 1:C, :]
            qg_scr[sl, :] = (q * jnp.exp2(G)).astype(BF16)
            kds.append((k * jnp.exp2(glast - G)).astype(BF16))
            eg_scr[c] = jnp.exp2(glast)
        for c, sl, a, kd in zip(chunks, sls, scores, kds):
            v = v_scr[sl, :]
            oi_scr[sl, :] = _dot(a.astype(BF16), v.astype(BF16))
            ds_scr[c] = _dot(v.T.astype(BF16), kd)
        return carry

    n_units = seq // (tm * HGRN_PROJ_TILES)
    assert tm * HGRN_PROJ_TILES == C * HGRN_UNROLL
    proj_body(0, 0)
    for t in range(1, n_units):
        proj_body(t, 0)
        intra_body(t - 1, 0)
    intra_body(n_units - 1, 0)

    def scan_body(c, st):
        st_scr[c] = st.astype(BF16)
        return st * eg_scr[c] + ds_scr[c]

    lax.fori_loop(0, n_chunks, scan_body, jnp.zeros((HEAD_DIM, HEAD_DIM), F32))

    def out_body(it, carry):
        chunks = [it * HGRN_OUT_UNROLL + u for u in range(HGRN_OUT_UNROLL)]
        sls = [pl.ds(pl.multiple_of(c * C, C), C) for c in chunks]
        outs = [oi_scr[sl, :] + _dot_nt(qg_scr[sl, :], st_scr[c]) for c, sl in zip(chunks, sls)]
        for sl, o in zip(sls, outs):
            y = _rms(o) * og_ref[...] * _silu(gate_scr[sl, :])
            y_ref[0, sl, :] = y.astype(BF16)
        return carry

    lax.fori_loop(0, n_chunks // HGRN_OUT_UNROLL, out_body, 0)


def _hgrn_mixer(h, w_heads, lb, o_gain):
    B, S, D = h.shape
    n_heads = D // HEAD_DIM
    per_head = w_heads.shape[1] // n_heads
    C = HGRN_CHUNK
    tri = jnp.asarray(np.tril(np.ones((C, C), np.float32)), BF16)
    masks = jnp.asarray(_hgrn_level_masks(C))
    row = lambda a: a[None, :]
    lane_spec = pl.BlockSpec((1, HEAD_DIM), lambda b, hd: (0, hd))
    seq_scr = pltpu.VMEM((S, HEAD_DIM), F32)
    return pl.pallas_call(
        functools.partial(_hgrn_kernel, seq=S),
        grid=(B, n_heads),
        in_specs=[
            pl.BlockSpec((1, S, D), lambda b, hd: (b, 0, 0)),
            pl.BlockSpec((D, per_head), lambda b, hd: (0, hd)),
            lane_spec, lane_spec, lane_spec,
            pl.BlockSpec((1, HEAD_DIM), lambda b, hd: (0, 0)),
            pl.BlockSpec((C, C), lambda b, hd: (0, 0)),
            pl.BlockSpec(masks.shape, lambda b, hd: (0, 0, 0)),
        ],
        out_specs=pl.BlockSpec((1, S, HEAD_DIM), lambda b, hd: (b, 0, hd)),
        out_shape=jax.ShapeDtypeStruct((B, S, D), BF16),
        scratch_shapes=[seq_scr] * 5 + [
            pltpu.VMEM((HGRN_UNROLL, C, HEAD_DIM), F32),
            seq_scr,
            pltpu.VMEM((S, HEAD_DIM), BF16),
            pltpu.VMEM((S // C, HEAD_DIM, HEAD_DIM), F32),
            pltpu.VMEM((S // C, 1, HEAD_DIM), F32),
            pltpu.VMEM((S // C, HEAD_DIM, HEAD_DIM), BF16),
        ],
        compiler_params=_params(("parallel", "arbitrary")),
        name="hgrn_mixer",
    )(h, w_heads, row(jnp.log(lb)), row(jnp.log1p(-lb)), row(1.0 - lb), row(o_gain), tri, masks)


def _regroup_kernel(*refs, n_rotary_parts):
    *in_refs, out_ref = refs
    for p, ref in enumerate(in_refs):
        blk = ref[...]
        if p < n_rotary_parts:
            blk = _pair_lanes(blk)
        out_ref[:, p * ref.shape[1]:(p + 1) * ref.shape[1]] = blk.astype(BF16)


def _split_in_weights(w_all, j, n_mix_parts, n_heads, n_rotary_parts=0):
    D = w_all.shape[1]
    gate_col = n_mix_parts * D + MEM_WIDTH
    strip = lambda col_block: pl.BlockSpec((None, D, HEAD_DIM), lambda hd: (j, 0, col_block + hd))
    specs = [strip(p * n_heads) for p in range(n_mix_parts)] + [strip(gate_col // HEAD_DIM)]
    per_head = len(specs) * HEAD_DIM
    heads = pl.pallas_call(
        functools.partial(_regroup_kernel, n_rotary_parts=n_rotary_parts),
        grid=(n_heads,),
        in_specs=specs,
        out_specs=pl.BlockSpec((D, per_head), lambda hd: (0, hd)),
        out_shape=jax.ShapeDtypeStruct((D, n_heads * per_head), BF16),
        compiler_params=_params(("parallel",)),
        name="regroup_heads",
    )(*([w_all] * len(specs)))
    wide = lambda col: pl.BlockSpec((None, D, MEM_WIDTH), lambda i: (j, 0, col // MEM_WIDTH))
    mem_cols = pl.pallas_call(
        functools.partial(_regroup_kernel, n_rotary_parts=0),
        grid=(1,),
        in_specs=[wide(n_mix_parts * D), wide(gate_col + D)],
        out_specs=pl.BlockSpec((D, 2 * MEM_WIDTH), lambda i: (0, 0)),
        out_shape=jax.ShapeDtypeStruct((D, 2 * MEM_WIDTH), BF16),
        compiler_params=_params(("arbitrary",)),
        name="regroup_mem",
    )(w_all, w_all)
    return heads, mem_cols


def kernel(x, mem, positions, norm_gain, w_in_a, q_gain_a, k_gain_a, w_out_a, w_in_b, lb_logits,
           o_gain_b, w_out_b, mem_norm_gain, w_mem_kv, mem_q_gain, mem_k_gain):
    B, S, D = x.shape
    depth = norm_gain.shape[0]
    n_heads = D // HEAD_DIM

    sm = jax.nn.softmax(lb_logits.astype(F32), axis=0)
    lower_bounds = jnp.cumsum(sm, axis=0) - sm[0:1]

    h, cos, sin = _prologue(x, positions, norm_gain[0])
    km, vm = _memkv(mem, mem_norm_gain, w_mem_kv, mem_k_gain)

    for l in range(depth):
        j = l // 2
        if l % 2 == 0:
            w_heads, w_mem = _split_in_weights(w_in_a, j, 3 * N_GROUPS, n_heads, 2 * N_GROUPS)
            y_mix = _attn_mixer(h, w_heads, cos, sin, _pair_lanes(q_gain_a[j]), _pair_lanes(k_gain_a[j]))
            w_out = w_out_a[j]
        else:
            w_heads, w_mem = _split_in_weights(w_in_b, j, 3, n_heads)
            y_mix = _hgrn_mixer(h, w_heads, lower_bounds[l], o_gain_b[j])
            w_out = w_out_b[j]
        next_gain = norm_gain[l + 1] if l + 1 < depth else None
        x, h = _mem_out(x, h, y_mix, w_mem, km, vm, l, mem_q_gain[l], w_out, next_gain)
    return x
```

```python
import functools
import math

import numpy as np
import jax
import jax.numpy as jnp
from jax import lax
from jax.experimental import pallas as pl
from jax.experimental.pallas import tpu as pltpu

F32 = jnp.float32
BF16 = jnp.bfloat16

HEAD_DIM = 128
ROT_DIM = HEAD_DIM // 4
ROT_HALF = ROT_DIM // 2
ROPE_THETA = 500000.0
DIL_PAIRS = ((128, 1), (512, 4), (2048, 16))
N_GROUPS = len(DIL_PAIRS)
N_BACK = 128
MEM_HEADS = 4
MEM_WIDTH = MEM_HEADS * HEAD_DIM
EPS = 1e-6
ATTN_SCALE = 1.0 / math.sqrt(HEAD_DIM)
NEG_BIG = -1e30
HGRN_CHUNK = 128
HGRN_UNROLL = 4
HGRN_PROJ_TILES = 2
HGRN_OUT_UNROLL = 8
LOG2E = 1.4426950408889634
ATTN_UNROLL = 8
ATTN_PROJ_TILES = 2
VMEM_LIMIT = 52 * 1024 * 1024

_NT = (((1,), (1,)), ((), ()))


def _dot(a, b):
    return jnp.dot(a, b, preferred_element_type=F32)


def _dot_nt(a, b):
    return lax.dot_general(a, b, _NT, preferred_element_type=F32)


def _rms(x):
    return x * lax.rsqrt(jnp.mean(x * x, axis=-1, keepdims=True) + EPS)


def _silu(g):
    return g * (1.0 / (1.0 + jnp.exp(-g)))


def _params(sem):
    return pltpu.CompilerParams(dimension_semantics=sem, vmem_limit_bytes=VMEM_LIMIT)


def _prologue_kernel(x_ref, pos_ref, g_ref, freq_ref, sgn_ref, h_ref, cos_ref, sin_ref):
    h_ref[0] = (_rms(x_ref[0]) * g_ref[...]).astype(BF16)
    ang = pos_ref[0] * freq_ref[...]
    cos_ref[0] = jnp.cos(ang)
    sin_ref[0] = jnp.sin(ang) * sgn_ref[...]


def _prologue(x, positions, gain0):
    B, S, D = x.shape
    ts = 512
    inv_freq = ROPE_THETA ** (-jnp.arange(0, ROT_DIM, 2, dtype=F32) / ROT_DIM)
    freq = _pair_lanes(jnp.concatenate([inv_freq, inv_freq, jnp.zeros((HEAD_DIM - ROT_DIM,), F32)]))[None, :]
    sgn = jnp.asarray(np.where(np.arange(HEAD_DIM) < ROT_HALF, -1.0, 1.0), F32)[None, :]
    pos = positions.astype(F32)[..., None]
    return pl.pallas_call(
        _prologue_kernel,
        grid=(B, S // ts),
        in_specs=[
            pl.BlockSpec((1, ts, D), lambda b, t: (b, t, 0)),
            pl.BlockSpec((1, ts, 1), lambda b, t: (b, t, 0)),
            pl.BlockSpec((1, D), lambda b, t: (0, 0)),
            pl.BlockSpec((1, HEAD_DIM), lambda b, t: (0, 0)),
            pl.BlockSpec((1, HEAD_DIM), lambda b, t: (0, 0)),
        ],
        out_specs=[
            pl.BlockSpec((1, ts, D), lambda b, t: (b, t, 0)),
            pl.BlockSpec((1, ts, HEAD_DIM), lambda b, t: (b, t, 0)),
            pl.BlockSpec((1, ts, HEAD_DIM), lambda b, t: (b, t, 0)),
        ],
        out_shape=[
            jax.ShapeDtypeStruct((B, S, D), BF16),
            jax.ShapeDtypeStruct((B, S, HEAD_DIM), F32),
            jax.ShapeDtypeStruct((B, S, HEAD_DIM), F32),
        ],
        compiler_params=_params(("parallel", "parallel")),
        name="prologue",
    )(x, pos, gain0[None, :], freq, sgn)


def _memkv_kernel(mem_ref, mg_ref, w_ref, kg_ref, km_ref, vm_ref):
    mn = (_rms(mem_ref[0]) * mg_ref[0]).astype(BF16)
    kv = _dot(mn, w_ref[0])
    for hd in range(MEM_HEADS):
        sl = slice(hd * HEAD_DIM, (hd + 1) * HEAD_DIM)
        km_ref[0, 0, :, sl] = (_rms(kv[:, sl]) * kg_ref[0]).astype(BF16)
    vm_ref[0, 0] = kv[:, MEM_WIDTH:].astype(BF16)


def _memkv(mem, mem_norm_gain, w_mem_kv, mem_k_gain):
    B, M, D = mem.shape
    depth = w_mem_kv.shape[0]
    out = jax.ShapeDtypeStruct((depth, B, M, MEM_WIDTH), BF16)
    return pl.pallas_call(
        _memkv_kernel,
        grid=(depth, B),
        in_specs=[
            pl.BlockSpec((1, M, D), lambda l, b: (b, 0, 0)),
            pl.BlockSpec((1, 1, D), lambda l, b: (l, 0, 0)),
            pl.BlockSpec((1, D, 2 * MEM_WIDTH), lambda l, b: (l, 0, 0)),
            pl.BlockSpec((1, 1, HEAD_DIM), lambda l, b: (l, 0, 0)),
        ],
        out_specs=[
            pl.BlockSpec((1, 1, M, MEM_WIDTH), lambda l, b: (l, b, 0, 0)),
            pl.BlockSpec((1, 1, M, MEM_WIDTH), lambda l, b: (l, b, 0, 0)),
        ],
        out_shape=[out, out],
        compiler_params=_params(("parallel", "parallel")),
        name="memkv",
    )(mem, mem_norm_gain[:, None, :], w_mem_kv.astype(BF16), mem_k_gain[:, None, :])


def _mem_out_kernel(x_ref, h_ref, ym_ref, wm_ref, km_ref, vm_ref, qg_ref, wo_ref, g_ref, xo_ref, *hn_ref):
    ht = h_ref[0]
    q_all = _dot(ht, wm_ref[:, :MEM_WIDTH])
    gate = _dot(ht, wm_ref[:, MEM_WIDTH:])
    y_heads = []
    for hd in range(MEM_HEADS):
        sl = slice(hd * HEAD_DIM, (hd + 1) * HEAD_DIM)
        qn = (_rms(q_all[:, sl]) * (qg_ref[...] * ATTN_SCALE)).astype(BF16)
        s = _dot_nt(qn, km_ref[0, 0, :, sl])
        m = jnp.max(s, axis=-1, keepdims=True)
        p = jnp.exp(s - m)
        den = jnp.sum(p, axis=-1, keepdims=True)
        o = _dot(p.astype(BF16), vm_ref[0, 0, :, sl]) * (1.0 / den)
        y_heads.append((o * _silu(gate[:, sl])).astype(BF16))
    y_mem = jnp.concatenate(y_heads, axis=1)
    mix_width = ym_ref.shape[2]
    acc = x_ref[0] + _dot(ym_ref[0], wo_ref[:mix_width, :]) + _dot(y_mem, wo_ref[mix_width:, :])
    xo_ref[0] = acc
    if hn_ref:
        hn_ref[0][0] = (_rms(acc) * g_ref[...]).astype(BF16)


def _mem_out(x, h, y_mix, w_mem, km, vm, layer, q_gain, w_out, next_gain):
    B, S, D = x.shape
    M = km.shape[2]
    ts = 512
    want_h = next_gain is not None
    gain = next_gain if want_h else jnp.ones((D,), F32)
    tile = pl.BlockSpec((1, ts, D), lambda b, t: (b, t, 0))
    mem_kv = pl.BlockSpec((1, 1, M, MEM_WIDTH), lambda b, t: (layer, b, 0, 0))
    fixed = lambda *shape: pl.BlockSpec(shape, lambda b, t: (0,) * len(shape))
    res = pl.pallas_call(
        _mem_out_kernel,
        grid=(B, S // ts),
        in_specs=[tile, tile, tile, fixed(D, 2 * MEM_WIDTH), mem_kv, mem_kv, fixed(1, HEAD_DIM),
                  fixed(D + MEM_WIDTH, D), fixed(1, D)],
        out_specs=[tile] + ([tile] if want_h else []),
        out_shape=[jax.ShapeDtypeStruct((B, S, D), F32)] + ([jax.ShapeDtypeStruct((B, S, D), BF16)] if want_h else []),
        compiler_params=_params(("parallel", "parallel")),
        name="mem_out",
    )(x, h, y_mix, w_mem, km, vm, q_gain[None, :], w_out.astype(BF16), gain[None, :])
    return (res[0], res[1]) if want_h else (res[0], None)


def _pair_lanes(x):
    half = HEAD_DIM // 2
    return jnp.concatenate([x[..., :ROT_HALF], x[..., ROT_DIM:half + ROT_HALF],
                            x[..., ROT_HALF:ROT_DIM], x[..., half + ROT_HALF:]], axis=-1)


def _rotary(x, cos, sin):
    return x * cos + pltpu.roll(x, HEAD_DIM // 2, 1) * sin


def _attn_bias():
    i = np.arange(N_BACK)[:, None]
    j = np.arange(2 * N_BACK)[None, :]
    band = (j >= i) & (j <= i + N_BACK)
    masks = np.stack([band, band & (j >= N_BACK), j <= i])
    return np.where(masks, 0.0, NEG_BIG).astype(np.float32)


def _attn_kernel(h_ref, w_ref, cos_ref, sin_ref, qg_ref, kg_ref, bias_ref, y_ref,
                 q0_scr, q1_scr, q2_scr, k0_scr, k1_scr, k2_scr, v0_scr, v1_scr, v2_scr,
                 o_scr, m_scr, den_scr, gate_scr, *, seq):
    q_scrs = (q0_scr, q1_scr, q2_scr)
    k_scrs = (k0_scr, k1_scr, k2_scr)
    v_scrs = (v0_scr, v1_scr, v2_scr)
    pads = tuple(N_BACK * d if seq // (N_BACK * d) > 1 else 0 for _, d in DIL_PAIRS)
    tm = 256

    for g in range(N_GROUPS):
        if pads[g]:
            k_scrs[g][pl.ds(0, pads[g]), :] = jnp.zeros((pads[g], HEAD_DIM), k_scrs[g].dtype)
            v_scrs[g][pl.ds(0, pads[g]), :] = jnp.zeros((pads[g], HEAD_DIM), v_scrs[g].dtype)

    def proj_body(t, carry):
        r0s = [pl.multiple_of((t * ATTN_PROJ_TILES + i) * tm, tm) for i in range(ATTN_PROJ_TILES)]
        hts = [h_ref[0, pl.ds(r0, tm), :] for r0 in r0s]
        for pair in range(5):
            w_pair = w_ref[:, pair * 256:(pair + 1) * 256]
            for r0, ht in zip(r0s, hts):
                cols = _dot(ht, w_pair)
                cos = cos_ref[0, pl.ds(r0, tm), :]
                sin = sin_ref[0, pl.ds(r0, tm), :]
                for half in range(2):
                    blk = 2 * pair + half
                    c = cols[:, half * HEAD_DIM:(half + 1) * HEAD_DIM]
                    if blk < 3:
                        g = blk
                        qn = _rms(c) * (qg_ref[g:g + 1, :] * (ATTN_SCALE * LOG2E))
                        q_scrs[g][pl.ds(r0, tm), :] = _rotary(qn, cos, sin).astype(q_scrs[g].dtype)
                    elif blk < 6:
                        g = blk - 3
                        kn = _rms(c) * kg_ref[g:g + 1, :]
                        k_scrs[g][pl.ds(pads[g] + r0, tm), :] = _rotary(kn, cos, sin).astype(k_scrs[g].dtype)
                    elif blk < 9:
                        g = blk - 6
                        v_scrs[g][pl.ds(pads[g] + r0, tm), :] = c.astype(v_scrs[g].dtype)
                    else:
                        gate_scr[pl.ds(r0, tm), :] = c
        return carry

    lax.fori_loop(0, seq // (tm * ATTN_PROJ_TILES), proj_body, 0)

    ones = jnp.ones((2 * N_BACK, HEAD_DIM), BF16)
    for g, (_, d) in enumerate(DIL_PAIRS):
        nb = seq // (N_BACK * d)
        nk = (2 if nb > 1 else 1) * N_BACK

        def rows(start, n, d=d):
            return pl.ds(start, n) if d == 1 else pl.ds(start, n, stride=d)

        for it in range(d * nb // ATTN_UNROLL):
            scores, outs = [], []
            for u in range(ATTN_UNROLL):
                r, c = divmod(it * ATTN_UNROLL + u, nb)
                start = c * (N_BACK * d) + r
                q = q_scrs[g][rows(start, N_BACK), :].astype(BF16)
                k = k_scrs[g][rows(start, nk), :].astype(BF16)
                if nb == 1:
                    bias = bias_ref[2, :, :N_BACK]
                else:
                    bias = bias_ref[1 if c == 0 else 0]
                scores.append((start, _dot_nt(q, k) + bias))
            for start, s in scores:
                m = jnp.max(s, axis=-1, keepdims=True)
                p = jnp.exp2(s - m).astype(BF16)
                v = v_scrs[g][rows(start, nk), :].astype(BF16)
                outs.append((start, _dot(p, jnp.concatenate([v, ones[:nk]], axis=1)), m))
            for start, od, m in outs:
                o_scr[g, rows(start, N_BACK), :] = od[:, :HEAD_DIM]
                m_scr[g, rows(start, N_BACK), :] = jnp.broadcast_to(m, (N_BACK, HEAD_DIM))
                den_scr[g, rows(start, N_BACK), :] = od[:, HEAD_DIM:]

    def merge_body(t, carry):
        r0 = pl.multiple_of(t * tm, tm)
        sl = pl.ds(r0, tm)
        m0, m1, m2 = m_scr[0, sl, :], m_scr[1, sl, :], m_scr[2, sl, :]
        mx = jnp.maximum(jnp.maximum(m0, m1), m2)
        w0, w1, w2 = jnp.exp2(m0 - mx), jnp.exp2(m1 - mx), jnp.exp2(m2 - mx)
        num = w0 * o_scr[0, sl, :] + w1 * o_scr[1, sl, :] + w2 * o_scr[2, sl, :]
        den = w0 * den_scr[0, sl, :] + w1 * den_scr[1, sl, :] + w2 * den_scr[2, sl, :]
        y_ref[0, sl, :] = (num * (1.0 / den) * _silu(gate_scr[sl, :])).astype(BF16)
        return carry

    lax.fori_loop(0, seq // tm, merge_body, 0)


def _attn_mixer(h, w_heads, cos, sin, q_gain, k_gain):
    B, S, D = h.shape
    n_heads = D // HEAD_DIM
    per_head = w_heads.shape[1] // n_heads
    assert S % (N_BACK * DIL_PAIRS[-1][1]) == 0
    pads = tuple(N_BACK * d if S // (N_BACK * d) > 1 else 0 for _, d in DIL_PAIRS)
    dts = [BF16 if d == 1 else F32 for _, d in DIL_PAIRS]
    q_scr = [pltpu.VMEM((S, HEAD_DIM), dt) for dt in dts]
    kv_scr = [pltpu.VMEM((p + S, HEAD_DIM), dt) for p, dt in zip(pads, dts)]
    grp_scr = pltpu.VMEM((N_GROUPS, S, HEAD_DIM), F32)
    bias = jnp.asarray(_attn_bias())
    return pl.pallas_call(
        functools.partial(_attn_kernel, seq=S),
        grid=(B, n_heads),
        in_specs=[
            pl.BlockSpec((1, S, D), lambda b, hd: (b, 0, 0)),
            pl.BlockSpec((D, per_head), lambda b, hd: (0, hd)),
            pl.BlockSpec((1, S, HEAD_DIM), lambda b, hd: (b, 0, 0)),
            pl.BlockSpec((1, S, HEAD_DIM), lambda b, hd: (b, 0, 0)),
            pl.BlockSpec((N_GROUPS, HEAD_DIM), lambda b, hd: (0, 0)),
            pl.BlockSpec((N_GROUPS, HEAD_DIM), lambda b, hd: (0, 0)),
            pl.BlockSpec(bias.shape, lambda b, hd: (0, 0, 0)),
        ],
        out_specs=pl.BlockSpec((1, S, HEAD_DIM), lambda b, hd: (b, 0, hd)),
        out_shape=jax.ShapeDtypeStruct((B, S, D), BF16),
        scratch_shapes=q_scr + kv_scr + kv_scr + [grp_scr, grp_scr, grp_scr,
                                                  pltpu.VMEM((S, HEAD_DIM), F32)],
        compiler_params=_params(("parallel", "arbitrary")),
        name="attn_mixer",
    )(h, w_heads, cos, sin, q_gain, k_gain, bias)


def _hgrn_level_masks(c):
    t = np.arange(c)[:, None]
    s = np.arange(c)[None, :]
    masks = [(t == s)]
    m = 1
    while m < c:
        masks.append((t // (2 * m) == s // (2 * m)) & (t % (2 * m) >= m) & (s % (2 * m) < m))
        m *= 2
    return np.stack(masks).astype(np.float32)


def _hgrn_kernel(h_ref, w_ref, la_ref, lc_ref, omlb_ref, og_ref, tri_ref, msk_ref, y_ref,
                 q_scr, lf_scr, k_scr, v_scr, gate_scr, g_scr, oi_scr, qg_scr, ds_scr,
                 eg_scr, st_scr, *, seq):
    tm = 256
    C = HGRN_CHUNK
    n_chunks = seq // C

    def proj_body(t, carry):
        sls = [pl.ds((t * HGRN_PROJ_TILES + i) * tm, tm) for i in range(HGRN_PROJ_TILES)]
        for sl in sls:
            ht = h_ref[0, sl, :]
            qf = _dot(ht, w_ref[:, :256])
            vg = _dot(ht, w_ref[:, 256:])
            f = qf[:, HEAD_DIM:]
            q_scr[sl, :] = qf[:, :HEAD_DIM]
            v_scr[sl, :] = vg[:, :HEAD_DIM]
            gate_scr[sl, :] = vg[:, HEAD_DIM:]
            sp = jnp.log(1.0 + jnp.exp2(jnp.abs(f) * -LOG2E))
            u = lc_ref[...] + (jnp.minimum(f, 0.0) - sp)
            la = la_ref[...]
            lf = jnp.maximum(la, u) + jnp.log(1.0 + jnp.exp2(jnp.abs(la - u) * -LOG2E))
            lf_scr[sl, :] = lf * LOG2E
            k_scr[sl, :] = omlb_ref[...] * jnp.exp(jnp.minimum(-f, 0.0) - sp)
        return carry

    row = lax.broadcasted_iota(jnp.int32, (C, HEAD_DIM), 0)
    n_levels = msk_ref.shape[0]

    def intra_body(it, carry):
        chunks = [it * HGRN_UNROLL + u for u in range(HGRN_UNROLL)]
        sls = [pl.ds(c * C, C) for c in chunks]
        tri = tri_ref[...]
        gs = []
        for sl in sls:
            lf = lf_scr[sl, :]
            hi = lf.astype(BF16)
            r1 = lf - hi.astype(F32)
            mid = r1.astype(BF16)
            lo = (r1 - mid.astype(F32)).astype(BF16)
            hm = _dot(tri, jnp.concatenate([hi, mid], axis=1))
            gs.append(hm[:, :HEAD_DIM] + hm[:, HEAD_DIM:] + _dot(tri, lo))
        scores, kds = [], []
        for u, (c, sl, G) in enumerate(zip(chunks, sls, gs)):
            g_scr[u] = G
            q = q_scr[sl, :]
            k = k_scr[sl, :]
            f = jnp.exp2(lf_scr[sl, :])
            a = msk_ref[0] * _dot_nt(q.astype(BF16), k.astype(BF16))
            x = jnp.where((row & 1) == 1, q * f, k).astype(BF16)
            a = a + msk_ref[1] * _dot_nt(x, x)
            r4 = row & 3
            dec = jnp.where(r4 == 3, f * pltpu.roll(f, 1, 0),
                            jnp.where(r4 == 2, f, jnp.where(r4 == 0, pltpu.roll(f, C - 1, 0), 1.0)))
            x = (jnp.where(r4 >= 2, q, k) * dec).astype(BF16)
            a = a + msk_ref[2] * _dot_nt(x, x)
            for lvl in range(3, n_levels):
                m = 2 ** (lvl - 1)
                ref_rows = [g_scr[u, pl.ds(i * 2 * m + m - 1, 1), :] for i in range(C // (2 * m))]
                if m >= 8:
                    parts = []
                    for i, ref_row in enumerate(ref_rows):
                        gref = jnp.broadcast_to(ref_row, (m, HEAD_DIM))
                        lo_rows = slice(i * 2 * m, i * 2 * m + m)
                        up_rows = slice(i * 2 * m + m, (i + 1) * 2 * m)
                        parts.append(k[lo_rows] * jnp.exp2(gref - G[lo_rows]))
                        parts.append(q[up_rows] * jnp.exp2(G[up_rows] - gref))
                    x = jnp.concatenate(parts, axis=0).astype(BF16)
                else:
                    gref = jnp.concatenate(
                        [jnp.broadcast_to(ref_row, (2 * m, HEAD_DIM)) for ref_row in ref_rows], axis=0)
                    upper = (row & m) != 0
                    d = G - gref
                    x = (jnp.where(upper, q, k) * jnp.exp2(jnp.where(upper, d, -d))).astype(BF16)
                a = a + msk_ref[lvl] * _dot_nt(x, x)
            scores.append(a)
            glast = g_scr[u, C - 1:C, :]
            qg_scr[sl, :] = (q * jnp.exp2(G)).astype(BF16)
            kds.append((k * jnp.exp2(glast - G)).astype(BF16))
            eg_scr[c] = jnp.exp2(glast)
        for c, sl, a, kd in zip(chunks, sls, scores, kds):
            v = v_scr[sl, :]
            oi_scr[sl, :] = _dot(a.astype(BF16), v.astype(BF16))
            ds_scr[c] = _dot(v.T.astype(BF16), kd)
        return carry

    n_units = seq // (tm * HGRN_PROJ_TILES)
    assert tm * HGRN_PROJ_TILES == C * HGRN_UNROLL
    proj_body(0, 0)
    for t in range(1, n_units):
        proj_body(t, 0)
        intra_body(t - 1, 0)
    intra_body(n_units - 1, 0)

    def scan_body(c, st):
        st_scr[c] = st.astype(BF16)
        return st * eg_scr[c] + ds_scr[c]

    lax.fori_loop(0, n_chunks, scan_body, jnp.zeros((HEAD_DIM, HEAD_DIM), F32))

    def out_body(it, carry):
        chunks = [it * HGRN_OUT_UNROLL + u for u in range(HGRN_OUT_UNROLL)]
        sls = [pl.ds(pl.multiple_of(c * C, C), C) for c in chunks]
        outs = [oi_scr[sl, :] + _dot_nt(qg_scr[sl, :], st_scr[c]) for c, sl in zip(chunks, sls)]
        for sl, o in zip(sls, outs):
            y = _rms(o) * og_ref[...] * _silu(gate_scr[sl, :])
            y_ref[0, sl, :] = y.astype(BF16)
        return carry

    lax.fori_loop(0, n_chunks // HGRN_OUT_UNROLL, out_body, 0)


def _hgrn_mixer(h, w_heads, lb, o_gain):
    B, S, D = h.shape
    n_heads = D // HEAD_DIM
    per_head = w_heads.shape[1] // n_heads
    C = HGRN_CHUNK
    tri = jnp.asarray(np.tril(np.ones((C, C), np.float32)), BF16)
    masks = jnp.asarray(_hgrn_level_masks(C))
    row = lambda a: a[None, :]
    lane_spec = pl.BlockSpec((1, HEAD_DIM), lambda b, hd: (0, hd))
    seq_scr = pltpu.VMEM((S, HEAD_DIM), F32)
    return pl.pallas_call(
        functools.partial(_hgrn_kernel, seq=S),
        grid=(B, n_heads),
        in_specs=[
            pl.BlockSpec((1, S, D), lambda b, hd: (b, 0, 0)),
            pl.BlockSpec((D, per_head), lambda b, hd: (0, hd)),
            lane_spec, lane_spec, lane_spec,
            pl.BlockSpec((1, HEAD_DIM), lambda b, hd: (0, 0)),
            pl.BlockSpec((C, C), lambda b, hd: (0, 0)),
            pl.BlockSpec(masks.shape, lambda b, hd: (0, 0, 0)),
        ],
        out_specs=pl.BlockSpec((1, S, HEAD_DIM), lambda b, hd: (b, 0, hd)),
        out_shape=jax.ShapeDtypeStruct((B, S, D), BF16),
        scratch_shapes=[seq_scr] * 5 + [
            pltpu.VMEM((HGRN_UNROLL, C, HEAD_DIM), F32),
            seq_scr,
            pltpu.VMEM((S, HEAD_DIM), BF16),
            pltpu.VMEM((S // C, HEAD_DIM, HEAD_DIM), F32),
            pltpu.VMEM((S // C, 1, HEAD_DIM), F32),
            pltpu.VMEM((S // C, HEAD_DIM, HEAD_DIM), BF16),
        ],
        compiler_params=_params(("parallel", "arbitrary")),
        name="hgrn_mixer",
    )(h, w_heads, row(jnp.log(lb)), row(jnp.log1p(-lb)), row(1.0 - lb), row(o_gain), tri, masks)


def _regroup_kernel(*refs, n_rotary_parts):
    *in_refs, out_ref = refs
    for p, ref in enumerate(in_refs):
        blk = ref[...]
        if p < n_rotary_parts:
            blk = _pair_lanes(blk)
        out_ref[:, p * ref.shape[1]:(p + 1) * ref.shape[1]] = blk.astype(BF16)


def _split_in_weights(w_all, j, n_mix_parts, n_heads, n_rotary_parts=0):
    D = w_all.shape[1]
    gate_col = n_mix_parts * D + MEM_WIDTH
    strip = lambda col_block: pl.BlockSpec((None, D, HEAD_DIM), lambda hd: (j, 0, col_block + hd))
    specs = [strip(p * n_heads) for p in range(n_mix_parts)] + [strip(gate_col // HEAD_DIM)]
    per_head = len(specs) * HEAD_DIM
    heads = pl.pallas_call(
        functools.partial(_regroup_kernel, n_rotary_parts=n_rotary_parts),
        grid=(n_heads,),
        in_specs=specs,
        out_specs=pl.BlockSpec((D, per_head), lambda hd: (0, hd)),
        out_shape=jax.ShapeDtypeStruct((D, n_heads * per_head), BF16),
        compiler_params=_params(("parallel",)),
        name="regroup_heads",
    )(*([w_all] * len(specs)))
    wide = lambda col: pl.BlockSpec((None, D, MEM_WIDTH), lambda i: (j, 0, col // MEM_WIDTH))
    mem_cols = pl.pallas_call(
        functools.partial(_regroup_kernel, n_rotary_parts=0),
        grid=(1,),
        in_specs=[wide(n_mix_parts * D), wide(gate_col + D)],
        out_specs=pl.BlockSpec((D, 2 * MEM_WIDTH), lambda i: (0, 0)),
        out_shape=jax.ShapeDtypeStruct((D, 2 * MEM_WIDTH), BF16),
        compiler_params=_params(("arbitrary",)),
        name="regroup_mem",
    )(w_all, w_all)
    return heads, mem_cols


def kernel(x, mem, positions, norm_gain, w_in_a, q_gain_a, k_gain_a, w_out_a, w_in_b, lb_logits,
           o_gain_b, w_out_b, mem_norm_gain, w_mem_kv, mem_q_gain, mem_k_gain):
    B, S, D = x.shape
    depth = norm_gain.shape[0]
    n_heads = D // HEAD_DIM

    sm = jax.nn.softmax(lb_logits.astype(F32), axis=0)
    lower_bounds = jnp.cumsum(sm, axis=0) - sm[0:1]

    h, cos, sin = _prologue(x, positions, norm_gain[0])
    km, vm = _memkv(mem, mem_norm_gain, w_mem_kv, mem_k_gain)

    for l in range(depth):
        j = l // 2
        if l % 2 == 0:
            w_heads, w_mem = _split_in_weights(w_in_a, j, 3 * N_GROUPS, n_heads, 2 * N_GROUPS)
            y_mix = _attn_mixer(h, w_heads, cos, sin, _pair_lanes(q_gain_a[j]), _pair_lanes(k_gain_a[j]))
            w_out = w_out_a[j]
        else:
            w_heads, w_mem = _split_in_weights(w_in_b, j, 3, n_heads)
            y_mix = _hgrn_mixer(h, w_heads, lower_bounds[l], o_gain_b[j])
            w_out = w_out_b[j]
        next_gain = norm_gain[l + 1] if l + 1 < depth else None
        x, h = _mem_out(x, h, y_mix, w_mem, km, vm, l, mem_q_gain[l], w_out, next_gain)
    return x
```

```python
import functools
import math

import numpy as np
import jax
import jax.numpy as jnp
from jax import lax
from jax.experimental import pallas as pl
from jax.experimental.pallas import tpu as pltpu

F32 = jnp.float32
BF16 = jnp.bfloat16

HEAD_DIM = 128
ROT_DIM = HEAD_DIM // 4
ROT_HALF = ROT_DIM // 2
ROPE_THETA = 500000.0
DIL_PAIRS = ((128, 1), (512, 4), (2048, 16))
N_GROUPS = len(DIL_PAIRS)
N_BACK = 128
MEM_HEADS = 4
MEM_WIDTH = MEM_HEADS * HEAD_DIM
EPS = 1e-6
ATTN_SCALE = 1.0 / math.sqrt(HEAD_DIM)
NEG_BIG = -1e30
HGRN_CHUNK = 128
HGRN_UNROLL = 8
HGRN_PROJ_TILES = 4
HGRN_OUT_UNROLL = 8
LOG2E = 1.4426950408889634
ATTN_UNROLL = 8
ATTN_PROJ_TILES = 2
VMEM_LIMIT = 52 * 1024 * 1024

_NT = (((1,), (1,)), ((), ()))


def _dot(a, b):
    return jnp.dot(a, b, preferred_element_type=F32)


def _dot_nt(a, b):
    return lax.dot_general(a, b, _NT, preferred_element_type=F32)


def _rms(x):
    return x * lax.rsqrt(jnp.mean(x * x, axis=-1, keepdims=True) + EPS)


def _silu(g):
    return g * (1.0 / (1.0 + jnp.exp(-g)))


def _params(sem):
    return pltpu.CompilerParams(dimension_semantics=sem, vmem_limit_bytes=VMEM_LIMIT)


def _prologue_kernel(x_ref, pos_ref, g_ref, freq_ref, sgn_ref, h_ref, cos_ref, sin_ref):
    h_ref[0] = (_rms(x_ref[0]) * g_ref[...]).astype(BF16)
    ang = pos_ref[0] * freq_ref[...]
    cos_ref[0] = jnp.cos(ang)
    sin_ref[0] = jnp.sin(ang) * sgn_ref[...]


def _prologue(x, positions, gain0):
    B, S, D = x.shape
    ts = 512
    inv_freq = ROPE_THETA ** (-jnp.arange(0, ROT_DIM, 2, dtype=F32) / ROT_DIM)
    freq = _pair_lanes(jnp.concatenate([inv_freq, inv_freq, jnp.zeros((HEAD_DIM - ROT_DIM,), F32)]))[None, :]
    sgn = jnp.asarray(np.where(np.arange(HEAD_DIM) < ROT_HALF, -1.0, 1.0), F32)[None, :]
    pos = positions.astype(F32)[..., None]
    return pl.pallas_call(
        _prologue_kernel,
        grid=(B, S // ts),
        in_specs=[
            pl.BlockSpec((1, ts, D), lambda b, t: (b, t, 0)),
            pl.BlockSpec((1, ts, 1), lambda b, t: (b, t, 0)),
            pl.BlockSpec((1, D), lambda b, t: (0, 0)),
            pl.BlockSpec((1, HEAD_DIM), lambda b, t: (0, 0)),
            pl.BlockSpec((1, HEAD_DIM), lambda b, t: (0, 0)),
        ],
        out_specs=[
            pl.BlockSpec((1, ts, D), lambda b, t: (b, t, 0)),
            pl.BlockSpec((1, ts, HEAD_DIM), lambda b, t: (b, t, 0)),
            pl.BlockSpec((1, ts, HEAD_DIM), lambda b, t: (b, t, 0)),
        ],
        out_shape=[
            jax.ShapeDtypeStruct((B, S, D), BF16),
            jax.ShapeDtypeStruct((B, S, HEAD_DIM), F32),
            jax.ShapeDtypeStruct((B, S, HEAD_DIM), F32),
        ],
        compiler_params=_params(("parallel", "parallel")),
        name="prologue",
    )(x, pos, gain0[None, :], freq, sgn)


def _memkv_kernel(mem_ref, mg_ref, w_ref, kg_ref, km_ref, vm_ref):
    mn = (_rms(mem_ref[0]) * mg_ref[0]).astype(BF16)
    kv = _dot(mn, w_ref[0])
    for hd in range(MEM_HEADS):
        sl = slice(hd * HEAD_DIM, (hd + 1) * HEAD_DIM)
        km_ref[0, 0, :, sl] = (_rms(kv[:, sl]) * kg_ref[0]).astype(BF16)
    vm_ref[0, 0] = kv[:, MEM_WIDTH:].astype(BF16)


def _memkv(mem, mem_norm_gain, w_mem_kv, mem_k_gain):
    B, M, D = mem.shape
    depth = w_mem_kv.shape[0]
    out = jax.ShapeDtypeStruct((depth, B, M, MEM_WIDTH), BF16)
    return pl.pallas_call(
        _memkv_kernel,
        grid=(depth, B),
        in_specs=[
            pl.BlockSpec((1, M, D), lambda l, b: (b, 0, 0)),
            pl.BlockSpec((1, 1, D), lambda l, b: (l, 0, 0)),
            pl.BlockSpec((1, D, 2 * MEM_WIDTH), lambda l, b: (l, 0, 0)),
            pl.BlockSpec((1, 1, HEAD_DIM), lambda l, b: (l, 0, 0)),
        ],
        out_specs=[
            pl.BlockSpec((1, 1, M, MEM_WIDTH), lambda l, b: (l, b, 0, 0)),
            pl.BlockSpec((1, 1, M, MEM_WIDTH), lambda l, b: (l, b, 0, 0)),
        ],
        out_shape=[out, out],
        compiler_params=_params(("parallel", "parallel")),
        name="memkv",
    )(mem, mem_norm_gain[:, None, :], w_mem_kv.astype(BF16), mem_k_gain[:, None, :])


def _mem_out_kernel(x_ref, h_ref, ym_ref, wm_ref, km_ref, vm_ref, qg_ref, wo_ref, g_ref, xo_ref, *hn_ref):
    ht = h_ref[0]
    q_all = _dot(ht, wm_ref[:, :MEM_WIDTH])
    gate = _dot(ht, wm_ref[:, MEM_WIDTH:])
    y_heads = []
    for hd in range(MEM_HEADS):
        sl = slice(hd * HEAD_DIM, (hd + 1) * HEAD_DIM)
        qn = (_rms(q_all[:, sl]) * (qg_ref[...] * ATTN_SCALE)).astype(BF16)
        s = _dot_nt(qn, km_ref[0, 0, :, sl])
        m = jnp.max(s, axis=-1, keepdims=True)
        p = jnp.exp(s - m)
        den = jnp.sum(p, axis=-1, keepdims=True)
        o = _dot(p.astype(BF16), vm_ref[0, 0, :, sl]) * (1.0 / den)
        y_heads.append((o * _silu(gate[:, sl])).astype(BF16))
    y_mem = jnp.concatenate(y_heads, axis=1)
    mix_width = ym_ref.shape[2]
    acc = x_ref[0] + _dot(ym_ref[0], wo_ref[:mix_width, :]) + _dot(y_mem, wo_ref[mix_width:, :])
    xo_ref[0] = acc
    if hn_ref:
        hn_ref[0][0] = (_rms(acc) * g_ref[...]).astype(BF16)


def _mem_out(x, h, y_mix, w_mem, km, vm, layer, q_gain, w_out, next_gain):
    B, S, D = x.shape
    M = km.shape[2]
    ts = 512
    want_h = next_gain is not None
    gain = next_gain if want_h else jnp.ones((D,), F32)
    tile = pl.BlockSpec((1, ts, D), lambda b, t: (b, t, 0))
    mem_kv = pl.BlockSpec((1, 1, M, MEM_WIDTH), lambda b, t: (layer, b, 0, 0))
    fixed = lambda *shape: pl.BlockSpec(shape, lambda b, t: (0,) * len(shape))
    res = pl.pallas_call(
        _mem_out_kernel,
        grid=(B, S // ts),
        in_specs=[tile, tile, tile, fixed(D, 2 * MEM_WIDTH), mem_kv, mem_kv, fixed(1, HEAD_DIM),
                  fixed(D + MEM_WIDTH, D), fixed(1, D)],
        out_specs=[tile] + ([tile] if want_h else []),
        out_shape=[jax.ShapeDtypeStruct((B, S, D), F32)] + ([jax.ShapeDtypeStruct((B, S, D), BF16)] if want_h else []),
        compiler_params=_params(("parallel", "parallel")),
        name="mem_out",
    )(x, h, y_mix, w_mem, km, vm, q_gain[None, :], w_out.astype(BF16), gain[None, :])
    return (res[0], res[1]) if want_h else (res[0], None)


def _pair_lanes(x):
    half = HEAD_DIM // 2
    return jnp.concatenate([x[..., :ROT_HALF], x[..., ROT_DIM:half + ROT_HALF],
                            x[..., ROT_HALF:ROT_DIM], x[..., half + ROT_HALF:]], axis=-1)


def _rotary(x, cos, sin):
    return x * cos + pltpu.roll(x, HEAD_DIM // 2, 1) * sin


def _attn_bias():
    i = np.arange(N_BACK)[:, None]
    j = np.arange(2 * N_BACK)[None, :]
    band = (j >= i) & (j <= i + N_BACK)
    masks = np.stack([band, band & (j >= N_BACK), j <= i])
    return np.where(masks, 0.0, NEG_BIG).astype(np.float32)


def _attn_kernel(h_ref, w_ref, cos_ref, sin_ref, qg_ref, kg_ref, bias_ref, y_ref,
                 q0_scr, q1_scr, q2_scr, k0_scr, k1_scr, k2_scr, v0_scr, v1_scr, v2_scr,
                 o_scr, m_scr, den_scr, gate_scr, *, seq):
    q_scrs = (q0_scr, q1_scr, q2_scr)
    k_scrs = (k0_scr, k1_scr, k2_scr)
    v_scrs = (v0_scr, v1_scr, v2_scr)
    pads = tuple(N_BACK * d if seq // (N_BACK * d) > 1 else 0 for _, d in DIL_PAIRS)
    tm = 256

    for g in range(N_GROUPS):
        if pads[g]:
            k_scrs[g][pl.ds(0, pads[g]), :] = jnp.zeros((pads[g], HEAD_DIM), k_scrs[g].dtype)
            v_scrs[g][pl.ds(0, pads[g]), :] = jnp.zeros((pads[g], HEAD_DIM), v_scrs[g].dtype)

    def proj_body(t, carry):
        r0s = [pl.multiple_of((t * ATTN_PROJ_TILES + i) * tm, tm) for i in range(ATTN_PROJ_TILES)]
        hts = [h_ref[0, pl.ds(r0, tm), :] for r0 in r0s]
        for pair in range(5):
            w_pair = w_ref[:, pair * 256:(pair + 1) * 256]
            for r0, ht in zip(r0s, hts):
                cols = _dot(ht, w_pair)
                cos = cos_ref[0, pl.ds(r0, tm), :]
                sin = sin_ref[0, pl.ds(r0, tm), :]
                for half in range(2):
                    blk = 2 * pair + half
                    c = cols[:, half * HEAD_DIM:(half + 1) * HEAD_DIM]
                    if blk < 3:
                        g = blk
                        qn = _rms(c) * (qg_ref[g:g + 1, :] * (ATTN_SCALE * LOG2E))
                        q_scrs[g][pl.ds(r0, tm), :] = _rotary(qn, cos, sin).astype(q_scrs[g].dtype)
                    elif blk < 6:
                        g = blk - 3
                        kn = _rms(c) * kg_ref[g:g + 1, :]
                        k_scrs[g][pl.ds(pads[g] + r0, tm), :] = _rotary(kn, cos, sin).astype(k_scrs[g].dtype)
                    elif blk < 9:
                        g = blk - 6
                        v_scrs[g][pl.ds(pads[g] + r0, tm), :] = c.astype(v_scrs[g].dtype)
                    else:
                        gate_scr[pl.ds(r0, tm), :] = c
        return carry

    lax.fori_loop(0, seq // (tm * ATTN_PROJ_TILES), proj_body, 0)

    ones = jnp.ones((2 * N_BACK, HEAD_DIM), BF16)
    for g, (_, d) in enumerate(DIL_PAIRS):
        nb = seq // (N_BACK * d)
        nk = (2 if nb > 1 else 1) * N_BACK

        def rows(start, n, d=d):
            return pl.ds(start, n) if d == 1 else pl.ds(start, n, stride=d)

        for it in range(d * nb // ATTN_UNROLL):
            scores, outs = [], []
            for u in range(ATTN_UNROLL):
                r, c = divmod(it * ATTN_UNROLL + u, nb)
                start = c * (N_BACK * d) + r
                q = q_scrs[g][rows(start, N_BACK), :].astype(BF16)
                k = k_scrs[g][rows(start, nk), :].astype(BF16)
                if nb == 1:
                    bias = bias_ref[2, :, :N_BACK]
                else:
                    bias = bias_ref[1 if c == 0 else 0]
                scores.append((start, _dot_nt(q, k) + bias))
            for start, s in scores:
                m = jnp.max(s, axis=-1, keepdims=True)
                p = jnp.exp2(s - m).astype(BF16)
                v = v_scrs[g][rows(start, nk), :].astype(BF16)
                outs.append((start, _dot(p, jnp.concatenate([v, ones[:nk]], axis=1)), m))
            for start, od, m in outs:
                o_scr[g, rows(start, N_BACK), :] = od[:, :HEAD_DIM]
                m_scr[g, rows(start, N_BACK), :] = jnp.broadcast_to(m, (N_BACK, HEAD_DIM))
                den_scr[g, rows(start, N_BACK), :] = od[:, HEAD_DIM:]

    def merge_body(t, carry):
        r0 = pl.multiple_of(t * tm, tm)
        sl = pl.ds(r0, tm)
        m0, m1, m2 = m_scr[0, sl, :], m_scr[1, sl, :], m_scr[2, sl, :]
        mx = jnp.maximum(jnp.maximum(m0, m1), m2)
        w0, w1, w2 = jnp.exp2(m0 - mx), jnp.exp2(m1 - mx), jnp.exp2(m2 - mx)
        num = w0 * o_scr[0, sl, :] + w1 * o_scr[1, sl, :] + w2 * o_scr[2, sl, :]
        den = w0 * den_scr[0, sl, :] + w1 * den_scr[1, sl, :] + w2 * den_scr[2, sl, :]
        y_ref[0, sl, :] = (num * (1.0 / den) * _silu(gate_scr[sl, :])).astype(BF16)
        return carry

    lax.fori_loop(0, seq // tm, merge_body, 0)


def _attn_mixer(h, w_heads, cos, sin, q_gain, k_gain):
    B, S, D = h.shape
    n_heads = D // HEAD_DIM
    per_head = w_heads.shape[1] // n_heads
    assert S % (N_BACK * DIL_PAIRS[-1][1]) == 0
    pads = tuple(N_BACK * d if S // (N_BACK * d) > 1 else 0 for _, d in DIL_PAIRS)
    dts = [BF16 if d == 1 else F32 for _, d in DIL_PAIRS]
    q_scr = [pltpu.VMEM((S, HEAD_DIM), dt) for dt in dts]
    kv_scr = [pltpu.VMEM((p + S, HEAD_DIM), dt) for p, dt in zip(pads, dts)]
    grp_scr = pltpu.VMEM((N_GROUPS, S, HEAD_DIM), F32)
    bias = jnp.asarray(_attn_bias())
    return pl.pallas_call(
        functools.partial(_attn_kernel, seq=S),
        grid=(B, n_heads),
        in_specs=[
            pl.BlockSpec((1, S, D), lambda b, hd: (b, 0, 0)),
            pl.BlockSpec((D, per_head), lambda b, hd: (0, hd)),
            pl.BlockSpec((1, S, HEAD_DIM), lambda b, hd: (b, 0, 0)),
            pl.BlockSpec((1, S, HEAD_DIM), lambda b, hd: (b, 0, 0)),
            pl.BlockSpec((N_GROUPS, HEAD_DIM), lambda b, hd: (0, 0)),
            pl.BlockSpec((N_GROUPS, HEAD_DIM), lambda b, hd: (0, 0)),
            pl.BlockSpec(bias.shape, lambda b, hd: (0, 0, 0)),
        ],
        out_specs=pl.BlockSpec((1, S, HEAD_DIM), lambda b, hd: (b, 0, hd)),
        out_shape=jax.ShapeDtypeStruct((B, S, D), BF16),
        scratch_shapes=q_scr + kv_scr + kv_scr + [grp_scr, grp_scr, grp_scr,
                                                  pltpu.VMEM((S, HEAD_DIM), F32)],
        compiler_params=_params(("parallel", "arbitrary")),
        name="attn_mixer",
    )(h, w_heads, cos, sin, q_gain, k_gain, bias)


def _hgrn_level_masks(c):
    t = np.arange(c)[:, None]
    s = np.arange(c)[None, :]
    masks = [(t == s)]
    m = 1
    while m < c:
        masks.append((t // (2 * m) == s // (2 * m)) & (t % (2 * m) >= m) & (s % (2 * m) < m))
        m *= 2
    return np.stack(masks).astype(np.float32)


def _hgrn_kernel(h_ref, w_ref, la_ref, lc_ref, omlb_ref, og_ref, tri_ref, msk_ref, y_ref,
                 q_scr, lf_scr, k_scr, v_scr, gate_scr, g_scr, oi_scr, qg_scr, ds_scr,
                 eg_scr, st_scr, *, seq):
    tm = 256
    C = HGRN_CHUNK
    n_chunks = seq // C

    def proj_body(t, carry):
        sls = [pl.ds((t * HGRN_PROJ_TILES + i) * tm, tm) for i in range(HGRN_PROJ_TILES)]
        for sl in sls:
            ht = h_ref[0, sl, :]
            qf = _dot(ht, w_ref[:, :256])
            vg = _dot(ht, w_ref[:, 256:])
            f = qf[:, HEAD_DIM:]
            q_scr[sl, :] = qf[:, :HEAD_DIM]
            v_scr[sl, :] = vg[:, :HEAD_DIM]
            gate_scr[sl, :] = vg[:, HEAD_DIM:]
            sp = jnp.log(1.0 + jnp.exp2(jnp.abs(f) * -LOG2E))
            u = lc_ref[...] + (jnp.minimum(f, 0.0) - sp)
            la = la_ref[...]
            lf = jnp.maximum(la, u) + jnp.log(1.0 + jnp.exp2(jnp.abs(la - u) * -LOG2E))
            lf_scr[sl, :] = lf * LOG2E
            k_scr[sl, :] = omlb_ref[...] * jnp.exp(jnp.minimum(-f, 0.0) - sp)
        return carry

    row = lax.broadcasted_iota(jnp.int32, (C, HEAD_DIM), 0)
    n_levels = msk_ref.shape[0]

    def intra_body(it, carry):
        chunks = [it * HGRN_UNROLL + u for u in range(HGRN_UNROLL)]
        sls = [pl.ds(c * C, C) for c in chunks]
        tri = tri_ref[...]
        gs = []
        for sl in sls:
            lf = lf_scr[sl, :]
            hi = lf.astype(BF16)
            r1 = lf - hi.astype(F32)
            mid = r1.astype(BF16)
            lo = (r1 - mid.astype(F32)).astype(BF16)
            hm = _dot(tri, jnp.concatenate([hi, mid], axis=1))
            gs.append(hm[:, :HEAD_DIM] + hm[:, HEAD_DIM:] + _dot(tri, lo))
        scores, kds = [], []
        for u, (c, sl, G) in enumerate(zip(chunks, sls, gs)):
            g_scr[u] = G
            q = q_scr[sl, :]
            k = k_scr[sl, :]
            f = jnp.exp2(lf_scr[sl, :])
            a = msk_ref[0] * _dot_nt(q.astype(BF16), k.astype(BF16))
            x = jnp.where((row & 1) == 1, q * f, k).astype(BF16)
            a = a + msk_ref[1] * _dot_nt(x, x)
            r4 = row & 3
            dec = jnp.where(r4 == 3, f * pltpu.roll(f, 1, 0),
                            jnp.where(r4 == 2, f, jnp.where(r4 == 0, pltpu.roll(f, C - 1, 0), 1.0)))
            x = (jnp.where(r4 >= 2, q, k) * dec).astype(BF16)
            a = a + msk_ref[2] * _dot_nt(x, x)
            for lvl in range(3, n_levels):
                m = 2 ** (lvl - 1)
                ref_rows = [g_scr[u, pl.ds(i * 2 * m + m - 1, 1), :] for i in range(C // (2 * m))]
                if m >= 8:
                    parts = []
                    for i, ref_row in enumerate(ref_rows):
                        gref = jnp.broadcast_to(ref_row, (m, HEAD_DIM))
                        lo_rows = slice(i * 2 * m, i * 2 * m + m)
                        up_rows = slice(i * 2 * m + m, (i + 1) * 2 * m)
                        parts.append(k[lo_rows] * jnp.exp2(gref - G[lo_rows]))
                        parts.append(q[up_rows] * jnp.exp2(G[up_rows] - gref))
                    x = jnp.concatenate(parts, axis=0).astype(BF16)
                else:
                    gref = jnp.concatenate(
                        [jnp.broadcast_to(ref_row, (2 * m, HEAD_DIM)) for ref_row in ref_rows], axis=0)
                    upper = (row & m) != 0
                    d = G - gref
                    x = (jnp.where(upper, q, k) * jnp.exp2(jnp.where(upper, d, -d))).astype(BF16)
                a = a + msk_ref[lvl] * _dot_nt(x, x)
            scores.append(a)
            glast = g_scr[u, C - 1:C, :]
            qg_scr[sl, :] = (q * jnp.exp2(G)).astype(BF16)
            kds.append((k * jnp.exp2(glast - G)).astype(BF16))
            eg_scr[c] = jnp.exp2(glast)
        for c, sl, a, kd in zip(chunks, sls, scores, kds):
            v = v_scr[sl, :]
            oi_scr[sl, :] = _dot(a.astype(BF16), v.astype(BF16))
            ds_scr[c] = _dot(v.T.astype(BF16), kd)
        return carry

    n_units = seq // (tm * HGRN_PROJ_TILES)
    assert tm * HGRN_PROJ_TILES == C * HGRN_UNROLL
    proj_body(0, 0)
    for t in range(1, n_units):
        proj_body(t, 0)
        intra_body(t - 1, 0)
    intra_body(n_units - 1, 0)

    def scan_body(c, st):
        st_scr[c] = st.astype(BF16)
        return st * eg_scr[c] + ds_scr[c]

    lax.fori_loop(0, n_chunks, scan_body, jnp.zeros((HEAD_DIM, HEAD_DIM), F32))

    def out_body(it, carry):
        chunks = [it * HGRN_OUT_UNROLL + u for u in range(HGRN_OUT_UNROLL)]
        sls = [pl.ds(pl.multiple_of(c * C, C), C) for c in chunks]
        outs = [oi_scr[sl, :] + _dot_nt(qg_scr[sl, :], st_scr[c]) for c, sl in zip(chunks, sls)]
        for sl, o in zip(sls, outs):
            y = _rms(o) * og_ref[...] * _silu(gate_scr[sl, :])
            y_ref[0, sl, :] = y.astype(BF16)
        return carry

    lax.fori_loop(0, n_chunks // HGRN_OUT_UNROLL, out_body, 0)


def _hgrn_mixer(h, w_heads, lb, o_gain):
    B, S, D = h.shape
    n_heads = D // HEAD_DIM
    per_head = w_heads.shape[1] // n_heads
    C = HGRN_CHUNK
    tri = jnp.asarray(np.tril(np.ones((C, C), np.float32)), BF16)
    masks = jnp.asarray(_hgrn_level_masks(C))
    row = lambda a: a[None, :]
    lane_spec = pl.BlockSpec((1, HEAD_DIM), lambda b, hd: (0, hd))
    seq_scr = pltpu.VMEM((S, HEAD_DIM), F32)
    return pl.pallas_call(
        functools.partial(_hgrn_kernel, seq=S),
        grid=(B, n_heads),
        in_specs=[
            pl.BlockSpec((1, S, D), lambda b, hd: (b, 0, 0)),
            pl.BlockSpec((D, per_head), lambda b, hd: (0, hd)),
            lane_spec, lane_spec, lane_spec,
            pl.BlockSpec((1, HEAD_DIM), lambda b, hd: (0, 0)),
            pl.BlockSpec((C, C), lambda b, hd: (0, 0)),
            pl.BlockSpec(masks.shape, lambda b, hd: (0, 0, 0)),
        ],
        out_specs=pl.BlockSpec((1, S, HEAD_DIM), lambda b, hd: (b, 0, hd)),
        out_shape=jax.ShapeDtypeStruct((B, S, D), BF16),
        scratch_shapes=[seq_scr] * 5 + [
            pltpu.VMEM((HGRN_UNROLL, C, HEAD_DIM), F32),
            seq_scr,
            pltpu.VMEM((S, HEAD_DIM), BF16),
            pltpu.VMEM((S // C, HEAD_DIM, HEAD_DIM), F32),
            pltpu.VMEM((S // C, 1, HEAD_DIM), F32),
            pltpu.VMEM((S // C, HEAD_DIM, HEAD_DIM), BF16),
        ],
        compiler_params=_params(("parallel", "arbitrary")),
        name="hgrn_mixer",
    )(h, w_heads, row(jnp.log(lb)), row(jnp.log1p(-lb)), row(1.0 - lb), row(o_gain), tri, masks)


def _regroup_kernel(*refs, n_rotary_parts):
    *in_refs, out_ref = refs
    for p, ref in enumerate(in_refs):
        blk = ref[...]
        if p < n_rotary_parts:
            blk = _pair_lanes(blk)
        out_ref[:, p * ref.shape[1]:(p + 1) * ref.shape[1]] = blk.astype(BF16)


def _split_in_weights(w_all, j, n_mix_parts, n_heads, n_rotary_parts=0):
    D = w_all.shape[1]
    gate_col = n_mix_parts * D + MEM_WIDTH
    strip = lambda col_block: pl.BlockSpec((None, D, HEAD_DIM), lambda hd: (j, 0, col_block + hd))
    specs = [strip(p * n_heads) for p in range(n_mix_parts)] + [strip(gate_col // HEAD_DIM)]
    per_head = len(specs) * HEAD_DIM
    heads = pl.pallas_call(
        functools.partial(_regroup_kernel, n_rotary_parts=n_rotary_parts),
        grid=(n_heads,),
        in_specs=specs,
        out_specs=pl.BlockSpec((D, per_head), lambda hd: (0, hd)),
        out_shape=jax.ShapeDtypeStruct((D, n_heads * per_head), BF16),
        compiler_params=_params(("parallel",)),
        name="regroup_heads",
    )(*([w_all] * len(specs)))
    wide = lambda col: pl.BlockSpec((None, D, MEM_WIDTH), lambda i: (j, 0, col // MEM_WIDTH))
    mem_cols = pl.pallas_call(
        functools.partial(_regroup_kernel, n_rotary_parts=0),
        grid=(1,),
        in_specs=[wide(n_mix_parts * D), wide(gate_col + D)],
        out_specs=pl.BlockSpec((D, 2 * MEM_WIDTH), lambda i: (0, 0)),
        out_shape=jax.ShapeDtypeStruct((D, 2 * MEM_WIDTH), BF16),
        compiler_params=_params(("arbitrary",)),
        name="regroup_mem",
    )(w_all, w_all)
    return heads, mem_cols


def kernel(x, mem, positions, norm_gain, w_in_a, q_gain_a, k_gain_a, w_out_a, w_in_b, lb_logits,
           o_gain_b, w_out_b, mem_norm_gain, w_mem_kv, mem_q_gain, mem_k_gain):
    B, S, D = x.shape
    depth = norm_gain.shape[0]
    n_heads = D // HEAD_DIM

    sm = jax.nn.softmax(lb_logits.astype(F32), axis=0)
    lower_bounds = jnp.cumsum(sm, axis=0) - sm[0:1]

    h, cos, sin = _prologue(x, positions, norm_gain[0])
    km, vm = _memkv(mem, mem_norm_gain, w_mem_kv, mem_k_gain)

    for l in range(depth):
        j = l // 2
        if l % 2 == 0:
            w_heads, w_mem = _split_in_weights(w_in_a, j, 3 * N_GROUPS, n_heads, 2 * N_GROUPS)
            y_mix = _attn_mixer(h, w_heads, cos, sin, _pair_lanes(q_gain_a[j]), _pair_lanes(k_gain_a[j]))
            w_out = w_out_a[j]
        else:
            w_heads, w_mem = _split_in_weights(w_in_b, j, 3, n_heads)
            y_mix = _hgrn_mixer(h, w_heads, lower_bounds[l], o_gain_b[j])
            w_out = w_out_b[j]
        next_gain = norm_gain[l + 1] if l + 1 < depth else None
        x, h = _mem_out(x, h, y_mix, w_mem, km, vm, l, mem_q_gain[l], w_out, next_gain)
    return x
```
